```python
import jax, jax.numpy as jnp
from jax import lax
import numpy as np

D_MODEL = 1024
BATCH = 8
SEQ = 2048
DEPTH = 4
DEC_BATCH = 128
DEC_SEQ = 8
PAST_LEN = 16384
PAGE_SIZE = 128

EXPAND = 2
W_MIX = EXPAND * D_MODEL
W_A = W_MIX // 2
W_B = W_MIX // 2
W_C = W_MIX
POOL_WINDOWS = (2, 4, 8, 16)
N_POOL_GROUPS = len(POOL_WINDOWS)
POOL_GROUP = W_A // N_POOL_GROUPS
POOL_HIST = max(POOL_WINDOWS) - 1
CONV_B_WIDTH = 31
CONV_B_HIST = CONV_B_WIDTH - 1
CONV_C_WIDTH = 3
CONV_C_HIST = CONV_C_WIDTH - 1
N_MEM = 256
N_XHEADS = 4
XHEAD_DIM = D_MODEL // N_XHEADS
N_EVEN = (DEPTH + 1) // 2
N_ODD = DEPTH // 2
EPS = 1e-6

kernel_name = 'hybrid_pool_conformer_shortconv_decoder_step'


def rmsnorm(x, g):
    xf = x.astype(jnp.float32)
    y = xf * lax.rsqrt(jnp.mean(xf * xf, axis=-1, keepdims=True) + EPS)
    return (y * g.astype(jnp.float32)).astype(x.dtype)


def layernorm(x, g, b):
    xf = x.astype(jnp.float32)
    mu = jnp.mean(xf, axis=-1, keepdims=True)
    var = jnp.mean(jnp.square(xf - mu), axis=-1, keepdims=True)
    y = (xf - mu) * lax.rsqrt(var + EPS) * g.astype(jnp.float32) + b.astype(jnp.float32)
    return y.astype(x.dtype)


def causal_dwconv(u_full, w):
    C = w.shape[1]
    return lax.conv_general_dilated(u_full, w[:, None, :].astype(u_full.dtype), window_strides=(1,),
                                    padding='VALID', dimension_numbers=('NWC', 'WIO', 'NWC'),
                                    feature_group_count=C)


def multiscale_pool_minus_self(u_full, start_pos):
    Bn, L, _ = u_full.shape
    T = L - POOL_HIST
    uf = u_full.astype(jnp.float32).reshape(Bn, L, N_POOL_GROUPS, POOL_GROUP)
    cs = jnp.concatenate([jnp.zeros_like(uf[:, :1]), jnp.cumsum(uf, axis=1)], axis=1)
    hi = cs[:, POOL_HIST + 1:]
    pos = start_pos + jnp.arange(T)
    outs = []
    for g, w in enumerate(POOL_WINDOWS):
        lo = cs[:, POOL_HIST + 1 - w: POOL_HIST + 1 - w + T, g]
        cnt = jnp.minimum(pos + 1, w).astype(jnp.float32)
        outs.append((hi[:, :, g] - lo) / cnt[None, :, None])
    mean = jnp.stack(outs, axis=2)
    return (mean - uf[:, POOL_HIST:]).astype(u_full.dtype)


def even_mixer(h, hist_pool, hist_conv, start_pos, w_in, pool_w, pool_scale, conv_w, conv_b, ln_g, ln_b, w_out):
    Bn, T, _ = h.shape
    z = jnp.einsum('btd,de->bte', h, w_in)
    a_in, a_gate, b_val, b_glu, b_gate = jnp.split(z, [W_A, 2 * W_A, 2 * W_A + W_B, 2 * W_A + 2 * W_B], axis=-1)
    a_full = jnp.concatenate([hist_pool.astype(a_in.dtype), a_in], axis=1)
    a = multiscale_pool_minus_self(a_full, start_pos)
    a = jnp.einsum('btgc,gce->btge', a, pool_w).reshape(Bn, T, W_A) * pool_scale
    a = a * jax.nn.silu(a_gate)
    v = b_val * jax.nn.sigmoid(b_glu)
    v_full = jnp.concatenate([hist_conv.astype(v.dtype), v], axis=1)
    c = causal_dwconv(v_full, conv_w) + conv_b
    c = jax.nn.silu(layernorm(c, ln_g, ln_b)) * jax.nn.silu(b_gate)
    y = jnp.einsum('bte,ed->btd', jnp.concatenate([a, c], axis=-1), w_out)
    return y, a_full[:, -POOL_HIST:], v_full[:, -CONV_B_HIST:]


def odd_mixer(h, hist_short, w_in, short_w, w_out):
    z = jnp.einsum('btd,de->bte', h, w_in)
    gb, gc, xv, g = jnp.split(z, 4, axis=-1)
    u = gc * xv
    u_full = jnp.concatenate([hist_short.astype(u.dtype), u], axis=1)
    y = gb * causal_dwconv(u_full, short_w) * jax.nn.silu(g)
    return jnp.einsum('bte,ed->btd', y, w_out), u_full[:, -CONV_C_HIST:]


def mem_kv(mem, g, w_k, w_v):
    Bn = mem.shape[0]
    m = rmsnorm(mem, g)
    k = jnp.einsum('bmd,de->bme', m, w_k).reshape(Bn, N_MEM, N_XHEADS, XHEAD_DIM)
    v = jnp.einsum('bmd,de->bme', m, w_v).reshape(Bn, N_MEM, N_XHEADS, XHEAD_DIM)
    return k, v


def cross_attend(h, k, v, w_q, w_o):
    Bn, T, _ = h.shape
    q = jnp.einsum('btd,de->bte', h, w_q).reshape(Bn, T, N_XHEADS, XHEAD_DIM)
    s = jnp.einsum('bthd,bmhd->bhtm', q, k.astype(q.dtype)).astype(jnp.float32) * (XHEAD_DIM ** -0.5)
    p = jax.nn.softmax(s, axis=-1).astype(q.dtype)
    o = jnp.einsum('bhtm,bmhd->bthd', p, v.astype(q.dtype)).reshape(Bn, T, N_XHEADS * XHEAD_DIM)
    return jnp.einsum('bte,ed->btd', o, w_o)


def run_trunk(x, hist_pool, hist_conv, hist_short, mem_k, mem_v, start_pos,
              norm_mix, w_in_even, pool_w, pool_scale, conv_w, conv_b, ln_g, ln_b, w_out_even,
              w_in_odd, short_w, w_out_odd, norm_x, w_q, w_o, final_norm):
    new_pool, new_conv, new_short = [], [], []
    for l in range(DEPTH):
        h = rmsnorm(x, norm_mix[l])
        if l % 2 == 0:
            e = l // 2
            y, sp, sc = even_mixer(h, hist_pool[e], hist_conv[e], start_pos, w_in_even[e], pool_w[e], pool_scale[e],
                                   conv_w[e], conv_b[e], ln_g[e], ln_b[e], w_out_even[e])
            new_pool.append(sp)
            new_conv.append(sc)
        else:
            o = l // 2
            y, ss = odd_mixer(h, hist_short[o], w_in_odd[o], short_w[o], w_out_odd[o])
            new_short.append(ss)
        x = x + y
        x = x + cross_attend(rmsnorm(x, norm_x[l]), mem_k[l], mem_v[l], w_q[l], w_o[l])
    return rmsnorm(x, final_norm), jnp.stack(new_pool), jnp.stack(new_conv), jnp.stack(new_short)


def setup_inputs(seed: int = 0) -> dict:
    key = jax.random.key(seed)
    ks = jax.random.split(key, 32)
    f32 = jnp.float32
    nrm = lambda k, shape, s: (jax.random.normal(k, shape, f32) * s).astype(f32)
    D = D_MODEL
    return {
        'x_prompt': nrm(ks[0], (BATCH, SEQ, D), 1.0),
        'x_sample': nrm(ks[1], (DEC_BATCH, DEC_SEQ, D), 1.0),
        'state_pool': nrm(ks[2], (N_EVEN, DEC_BATCH, POOL_HIST, W_A), 1.0),
        'state_conv': nrm(ks[3], (N_EVEN, DEC_BATCH, CONV_B_HIST, W_B), 0.5),
        'state_short': nrm(ks[4], (N_ODD, DEC_BATCH, CONV_C_HIST, W_C), 1.0),
        'cache_mem_k': nrm(ks[5], (DEPTH, DEC_BATCH, N_MEM, N_XHEADS, XHEAD_DIM), 1.0),
        'cache_mem_v': nrm(ks[6], (DEPTH, DEC_BATCH, N_MEM, N_XHEADS, XHEAD_DIM), 1.0),
        'mem_prompt': nrm(ks[7], (BATCH, N_MEM, D), 1.0),
        'norm_mix': 1.0 + nrm(ks[8], (DEPTH, D), 0.02),
        'w_in_even': nrm(ks[9], (N_EVEN, D, 2 * W_A + 3 * W_B), D ** -0.5),
        'pool_w': nrm(ks[10], (N_EVEN, N_POOL_GROUPS, POOL_GROUP, POOL_GROUP), POOL_GROUP ** -0.5),
        'pool_scale': 1.0 + nrm(ks[11], (N_EVEN, W_A), 0.02),
        'conv_w': nrm(ks[12], (N_EVEN, CONV_B_WIDTH, W_B), CONV_B_WIDTH ** -0.5),
        'conv_b': nrm(ks[13], (N_EVEN, W_B), 0.02),
        'ln_g': 1.0 + nrm(ks[14], (N_EVEN, W_B), 0.02),
        'ln_b': nrm(ks[15], (N_EVEN, W_B), 0.02),
        'w_out_even': nrm(ks[16], (N_EVEN, W_A + W_B, D), (W_A + W_B) ** -0.5),
        'w_in_odd': nrm(ks[17], (N_ODD, D, 4 * W_C), D ** -0.5),
        'short_w': nrm(ks[18], (N_ODD, CONV_C_WIDTH, W_C), CONV_C_WIDTH ** -0.5),
        'w_out_odd': nrm(ks[19], (N_ODD, W_C, D), W_C ** -0.5),
        'norm_x': 1.0 + nrm(ks[20], (DEPTH, D), 0.02),
        'norm_mem': 1.0 + nrm(ks[21], (DEPTH, D), 0.02),
        'w_q': nrm(ks[22], (DEPTH, D, N_XHEADS * XHEAD_DIM), D ** -0.5),
        'w_k': nrm(ks[23], (DEPTH, D, N_XHEADS * XHEAD_DIM), D ** -0.5),
        'w_v': nrm(ks[24], (DEPTH, D, N_XHEADS * XHEAD_DIM), D ** -0.5),
        'w_o': nrm(ks[25], (DEPTH, N_XHEADS * XHEAD_DIM, D), (N_XHEADS * XHEAD_DIM) ** -0.5),
        'final_norm': 1.0 + nrm(ks[26], (D,), 0.02),
    }


def reference(x_prompt, x_sample, state_pool, state_conv, state_short, cache_mem_k, cache_mem_v, mem_prompt,
              norm_mix, w_in_even, pool_w, pool_scale, conv_w, conv_b, ln_g, ln_b, w_out_even,
              w_in_odd, short_w, w_out_odd, norm_x, norm_mem, w_q, w_k, w_v, w_o, final_norm):
    dt = x_prompt.dtype
    kv = [mem_kv(mem_prompt, norm_mem[l], w_k[l], w_v[l]) for l in range(DEPTH)]
    mem_k_p = jnp.stack([k for k, _ in kv])
    mem_v_p = jnp.stack([v for _, v in kv])
    zp = jnp.zeros((N_EVEN, BATCH, POOL_HIST, W_A), dt)
    zc = jnp.zeros((N_EVEN, BATCH, CONV_B_HIST, W_B), dt)
    zs = jnp.zeros((N_ODD, BATCH, CONV_C_HIST, W_C), dt)
    y_prompt, pool_p, conv_p, short_p = run_trunk(
        x_prompt, zp, zc, zs, mem_k_p, mem_v_p, 0,
        norm_mix, w_in_even, pool_w, pool_scale, conv_w, conv_b, ln_g, ln_b, w_out_even,
        w_in_odd, short_w, w_out_odd, norm_x, w_q, w_o, final_norm)
    y_sample, pool_s, conv_s, short_s = run_trunk(
        x_sample, state_pool, state_conv, state_short, cache_mem_k, cache_mem_v, PAST_LEN,
        norm_mix, w_in_even, pool_w, pool_scale, conv_w, conv_b, ln_g, ln_b, w_out_even,
        w_in_odd, short_w, w_out_odd, norm_x, w_q, w_o, final_norm)
    return (y_prompt, y_sample, pool_p, conv_p, short_p, mem_k_p, mem_v_p, pool_s, conv_s, short_s)
```

```python
import functools

import jax
import jax.numpy as jnp
from jax import lax
from jax.experimental import pallas as pl
from jax.experimental.pallas import tpu as pltpu

F32 = jnp.float32
BF16 = jnp.bfloat16

EPS = 1e-6
POOL_WINDOWS = (2, 4, 8, 16)
POOL_HIST = max(POOL_WINDOWS) - 1
CONV_TAPS = 31
CONV_HIST = CONV_TAPS - 1
SHORT_TAPS = 3
SHORT_HIST = SHORT_TAPS - 1
N_HEADS = 4
PAST_LEN = 16384

LANES = 128
SUBLANES = 8
VMEM_LIMIT_BYTES = 56 * 1024 * 1024

TILE_ROWS = 256
NORM_ROWS = 32
XATTN_SEQS = 8


def _mm(a, b):
    return jnp.dot(a, b, preferred_element_type=F32)


def _mm_nt(a, b):
    return lax.dot_general(a, b, (((1,), (1,)), ((), ())), preferred_element_type=F32)


def _silu(x):
    return x * jax.nn.sigmoid(x)


def _rmsnorm(x, g):
    ms = jnp.mean(x * x, axis=-1, keepdims=True)
    return x * lax.rsqrt(ms + EPS) * g


def _softmax(s):
    e = jnp.exp(s - jnp.max(s, axis=-1, keepdims=True))
    return e / jnp.sum(e, axis=-1, keepdims=True)


def _norm_rows(dst_ref, src_ref, g_ref, dtype):
    rows = src_ref.shape[0]
    for r0 in range(0, rows, NORM_ROWS):
        rs = slice(r0, r0 + NORM_ROWS)
        dst_ref[rs, :] = _rmsnorm(src_ref[rs, :], g_ref[...]).astype(dtype)


def _window_rows(hist_ref, new_ref, start, n_rows, cols):
    hist_rows = hist_ref.shape[0]
    n_hist = min(max(hist_rows - start, 0), n_rows)
    parts = []
    if n_hist > 0:
        parts.append(hist_ref[start:start + n_hist, cols])
    if n_hist < n_rows:
        s = start + n_hist - hist_rows
        parts.append(new_ref[s:s + n_rows - n_hist, cols])
    return parts[0] if len(parts) == 1 else jnp.concatenate(parts, axis=0)


def _last_rows(dst_ref, hist_ref, new_ref):
    n, tm = dst_ref.shape[0], new_ref.shape[0]
    if tm >= n:
        dst_ref[...] = new_ref[tm - n:tm, :]
    else:
        dst_ref[0:n - tm, :] = hist_ref[tm:n, :]
        dst_ref[n - tm:n, :] = new_ref[...]


def _cross_attend(x1_scr, nx_ref, wq_ref, k_ref, v_ref, wo_ref, h_scr, q_scr, oh_scr, nb, tt):
    d_head = k_ref.shape[-1]
    chunks = d_head // LANES
    scale = d_head ** -0.5
    _norm_rows(h_scr, x1_scr, nx_ref, BF16)
    for hd in range(N_HEADS):
        q = _mm(h_scr[...], wq_ref[:, hd * d_head:(hd + 1) * d_head]) * scale
        for c in range(chunks):
            q_scr[hd * chunks + c] = q[:, c * LANES:(c + 1) * LANES]
    for b in range(nb):
        rows = pl.ds(b, tt, stride=nb)
        for hd in range(N_HEADS):
            qb = jnp.concatenate([q_scr[hd * chunks + c, rows, :] for c in range(chunks)], axis=1)
            p = _softmax(_mm_nt(qb.astype(BF16), k_ref[b, hd]))
            o = _mm(p.astype(BF16), v_ref[b, hd])
            for c in range(chunks):
                oh_scr[hd * chunks + c, rows, :] = o[:, c * LANES:(c + 1) * LANES]
    o_all = jnp.concatenate([oh_scr[j] for j in range(N_HEADS * chunks)], axis=1)
    return x1_scr[...] + _mm(o_all.astype(BF16), wo_ref[...])


def _finish(o_ref, out, x1_scr, fn_ref):
    if fn_ref is None:
        o_ref[...] = out
    else:
        x1_scr[...] = out
        _norm_rows(o_ref, x1_scr, fn_ref, F32)


def _even_kernel(*refs, nb, tt, start_pos, carry, attn, final):
    tm = nb * tt
    it = iter(refs)
    x_ref, hp_ref, hc_ref = next(it), next(it), next(it)
    nmix_ref, win_ref, poolw_ref, pscale_ref = next(it), next(it), next(it), next(it)
    convw_ref, convb_ref, lng_ref, lnb_ref, wout_ref = next(it), next(it), next(it), next(it), next(it)
    if attn:
        nx_ref, wq_ref, k_ref, v_ref, wo_ref = next(it), next(it), next(it), next(it), next(it)
    fn_ref = next(it) if final else None
    o_ref, newp_ref, newc_ref = next(it), next(it), next(it)
    h_scr, pool_scr, vnew_scr, t_scr, c_scr, ac_scr, x1_scr = (next(it) for _ in range(7))
    hc_scr = next(it) if carry else None
    if attn:
        q_scr, oh_scr = next(it), next(it)

    i = pl.program_id(0)
    n_steps = pl.num_programs(0)
    w_a = pscale_ref.shape[-1]
    w_b = convb_ref.shape[-1]
    hp_rows = POOL_HIST * nb
    pool_group = w_a // len(POOL_WINDOWS)

    if carry:
        @pl.when(i == 0)
        def _():
            pool_scr[0:hp_rows, :] = hp_ref[...]
            hc_scr[...] = hc_ref[...]
        hist_c = hc_scr
    else:
        pool_scr[0:hp_rows, :] = hp_ref[...]
        hist_c = hc_ref

    _norm_rows(h_scr, x_ref, nmix_ref, BF16)

    pool_scr[hp_rows:hp_rows + tm, :] = _mm(h_scr[...], win_ref[:, 0:w_a])
    t_scr[...] = _mm(h_scr[...], win_ref[:, w_a:2 * w_a])

    row = lax.broadcasted_iota(jnp.int32, (tm, LANES), 0)
    step = row >> (nb.bit_length() - 1)
    pos1 = step + (start_pos + 1 + (i * tt if carry else 0))
    for g, w in enumerate(POOL_WINDOWS):
        cnt = jnp.minimum(pos1, w).astype(F32)
        a_parts = []
        for c0 in range(g * pool_group, (g + 1) * pool_group, LANES):
            e = pool_scr[(POOL_HIST + 1 - w) * nb:hp_rows + tm, c0:c0 + LANES]
            s, sh = e, 1
            while sh < w:
                n = s.shape[0]
                s = s[sh * nb:, :] + s[:n - sh * nb, :]
                sh *= 2
            a_parts.append((s / cnt - e[(w - 1) * nb:, :]).astype(BF16))
        cs = slice(g * pool_group, (g + 1) * pool_group)
        pa = _mm(jnp.concatenate(a_parts, axis=1), poolw_ref[g])
        ac_scr[:, cs] = (pa * pscale_ref[:, cs] * _silu(t_scr[:, cs])).astype(BF16)

    bval = _mm(h_scr[...], win_ref[:, 2 * w_a:2 * w_a + w_b])
    bglu = _mm(h_scr[...], win_ref[:, 2 * w_a + w_b:2 * w_a + 2 * w_b])
    vnew_scr[...] = bval * jax.nn.sigmoid(bglu)
    t_scr[...] = _mm(h_scr[...], win_ref[:, 2 * w_a + 2 * w_b:2 * w_a + 3 * w_b])
    for c0 in range(0, w_b, LANES):
        cs = slice(c0, c0 + LANES)
        acc = jnp.broadcast_to(convb_ref[:, cs], (tm, LANES))
        for k in range(CONV_TAPS):
            acc = acc + _window_rows(hist_c, vnew_scr, k * nb, tm, cs) * convw_ref[k:k + 1, cs]
        c_scr[:, cs] = acc
    for r0 in range(0, tm, NORM_ROWS):
        rs = slice(r0, r0 + NORM_ROWS)
        c = c_scr[rs, :]
        d = c - jnp.mean(c, axis=-1, keepdims=True)
        var = jnp.mean(d * d, axis=-1, keepdims=True)
        y = d * lax.rsqrt(var + EPS) * lng_ref[...] + lnb_ref[...]
        ac_scr[rs, w_a:w_a + w_b] = (_silu(y) * _silu(t_scr[rs, :])).astype(BF16)

    x1 = x_ref[...] + _mm(ac_scr[...], wout_ref[...])

    if carry:
        _last_rows(hc_scr, hc_scr, vnew_scr)
        hp_new = pool_scr[tm:tm + hp_rows, :]
        pool_scr[0:hp_rows, :] = hp_new

        @pl.when(i == n_steps - 1)
        def _():
            newp_ref[...] = pool_scr[0:hp_rows, :]
            newc_ref[...] = hc_scr[...]
    else:
        newp_ref[...] = pool_scr[tm:tm + hp_rows, :]
        _last_rows(newc_ref, hc_ref, vnew_scr)

    if attn:
        x1_scr[...] = x1
        x1 = _cross_attend(x1_scr, nx_ref, wq_ref, k_ref, v_ref, wo_ref, h_scr, q_scr, oh_scr, nb, tt)
    _finish(o_ref, x1, x1_scr, fn_ref)


def _odd_kernel(*refs, nb, tt, carry, attn, final):
    tm = nb * tt
    it = iter(refs)
    x_ref, hs_ref = next(it), next(it)
    nmix_ref, win_ref, shortw_ref, wout_ref = next(it), next(it), next(it), next(it)
    if attn:
        nx_ref, wq_ref, k_ref, v_ref, wo_ref = next(it), next(it), next(it), next(it), next(it)
    fn_ref = next(it) if final else None
    o_ref, news_ref = next(it), next(it)
    h_scr, unew_scr, gate_scr, y_scr, x1_scr = (next(it) for _ in range(5))
    hs_scr = next(it) if carry else None
    if attn:
        q_scr, oh_scr = next(it), next(it)

    i = pl.program_id(0)
    n_steps = pl.num_programs(0)
    w_c = shortw_ref.shape[-1]
    d_model = x_ref.shape[-1]

    if carry:
        @pl.when(i == 0)
        def _():
            hs_scr[...] = hs_ref[...]
        hist = hs_scr
    else:
        hist = hs_ref

    _norm_rows(h_scr, x_ref, nmix_ref, BF16)

    for c0 in range(0, w_c, d_model):
        cs = slice(c0, c0 + d_model)
        gc = _mm(h_scr[...], win_ref[:, w_c + c0:w_c + c0 + d_model])
        xv = _mm(h_scr[...], win_ref[:, 2 * w_c + c0:2 * w_c + c0 + d_model])
        unew_scr[:, cs] = gc * xv
    for c0 in range(0, w_c, d_model):
        cs = slice(c0, c0 + d_model)
        gb = _mm(h_scr[...], win_ref[:, c0:c0 + d_model])
        gg = _mm(h_scr[...], win_ref[:, 3 * w_c + c0:3 * w_c + c0 + d_model])
        gate_scr[:, cs] = gb * _silu(gg)
    for c0 in range(0, w_c, LANES):
        cs = slice(c0, c0 + LANES)
        acc = _window_rows(hist, unew_scr, 0, tm, cs) * shortw_ref[0:1, cs]
        for k in range(1, SHORT_TAPS):
            acc = acc + _window_rows(hist, unew_scr, k * nb, tm, cs) * shortw_ref[k:k + 1, cs]
        y_scr[:, cs] = (gate_scr[:, cs] * acc).astype(BF16)

    x1 = x_ref[...] + _mm(y_scr[...], wout_ref[...])

    if carry:
        _last_rows(hs_scr, hs_scr, unew_scr)

        @pl.when(i == n_steps - 1)
        def _():
            news_ref[...] = hs_scr[...]
    else:
        _last_rows(news_ref, hs_ref, unew_scr)

    if attn:
        x1_scr[...] = x1
        x1 = _cross_attend(x1_scr, nx_ref, wq_ref, k_ref, v_ref, wo_ref, h_scr, q_scr, oh_scr, nb, tt)
    _finish(o_ref, x1, x1_scr, fn_ref)


def _xattn_sample_kernel(*refs, nbs, final):
    it = iter(refs)
    x_ref, nx_ref, wq_ref, k_ref, v_ref, wo_ref = (next(it) for _ in range(6))
    fn_ref = next(it) if final else None
    o_ref = next(it)
    h_scr, q_scr, oh_scr, x1_scr = (next(it) for _ in range(4))

    c = pl.program_id(1)
    tt = x_ref.shape[0] // nbs
    seqs = k_ref.shape[0]
    d_head = wq_ref.shape[-1] // N_HEADS
    chunks = d_head // LANES
    per_mem = chunks * N_HEADS
    n_mem = k_ref.shape[1] // per_mem
    scale = d_head ** -0.5

    @pl.when(c == 0)
    def _():
        _norm_rows(h_scr, x_ref, nx_ref, BF16)
        for hd in range(N_HEADS):
            q = _mm(h_scr[...], wq_ref[:, hd * d_head:(hd + 1) * d_head]) * scale
            for cc in range(chunks):
                q_scr[hd * chunks + cc] = q[:, cc * LANES:(cc + 1) * LANES]

    def head_rows(ref, bl, hd):
        parts = [ref[bl, pl.ds(cc * N_HEADS + hd, n_mem, stride=per_mem), :] for cc in range(chunks)]
        return jnp.concatenate(parts, axis=1).astype(BF16)

    for bl in range(seqs):
        rows = pl.ds(c * seqs + bl, tt, stride=nbs)
        for hd in range(N_HEADS):
            qb = jnp.concatenate([q_scr[hd * chunks + cc, rows, :] for cc in range(chunks)], axis=1)
            p = _softmax(_mm_nt(qb.astype(BF16), head_rows(k_ref, bl, hd)))
            o = _mm(p.astype(BF16), head_rows(v_ref, bl, hd))
            for cc in range(chunks):
                oh_scr[hd * chunks + cc, rows, :] = o[:, cc * LANES:(cc + 1) * LANES]

    @pl.when(c == pl.num_programs(1) - 1)
    def _():
        o_all = jnp.concatenate([oh_scr[j] for j in range(N_HEADS * chunks)], axis=1)
        out = x_ref[...] + _mm(o_all.astype(BF16), wo_ref[...])
        _finish(o_ref, out, x1_scr, fn_ref)


def _memory_kv_kernel(mem_ref, nmem_ref, wk_ref, wv_ref, kr_ref, vr_ref, kb_ref, vb_ref, h_scr):
    n_mem = mem_ref.shape[0]
    d_head = kb_ref.shape[-1]
    chunks = d_head // LANES
    per_mem = chunks * N_HEADS
    _norm_rows(h_scr, mem_ref, nmem_ref, BF16)
    for w_ref, r_ref, b_ref in ((wk_ref, kr_ref, kb_ref), (wv_ref, vr_ref, vb_ref)):
        for hd in range(N_HEADS):
            kh = _mm(h_scr[...], w_ref[:, hd * d_head:(hd + 1) * d_head])
            b_ref[hd] = kh.astype(BF16)
            for cc in range(chunks):
                r_ref[pl.ds(cc * N_HEADS + hd, n_mem, stride=per_mem), :] = kh[:, cc * LANES:(cc + 1) * LANES]


def _const_spec(shape, layer=None):
    if layer is None:
        return pl.BlockSpec(shape, lambda *_: (0,) * len(shape), pipeline_mode=pl.Buffered(1))
    return pl.BlockSpec((None,) + shape, lambda *_: (layer,) + (0,) * len(shape), pipeline_mode=pl.Buffered(1))


def _row_spec(rows, cols):
    return pl.BlockSpec((rows, cols), lambda i: (i, 0))


def _params(n_axes):
    return pltpu.CompilerParams(dimension_semantics=("arbitrary",) * n_axes, vmem_limit_bytes=VMEM_LIMIT_BYTES)


def _attn_inputs(p, l, kv, nb):
    kb, vb = kv
    d = p["w_q"].shape[-1]
    kv_block = (nb,) + kb.shape[2:]
    args = [p["norm_x"], p["w_q"], kb, vb, p["w_o"]]
    specs = [_const_spec((1, d), l), _const_spec((d, d), l), _const_spec(kv_block, l), _const_spec(kv_block, l),
             _const_spec((d, d), l)]
    return args, specs


def _attn_scratch(tm, d):
    return [pltpu.VMEM((d // LANES, tm, LANES), F32), pltpu.VMEM((d // LANES, tm, LANES), F32)]


def _even_layer(x, hp, hc, p, e, l, *, nb, tt, start_pos, carry, kv=None, final_norm=None):
    rows, d = x.shape
    tm = nb * tt
    w_a, w_b = p["pool_scale"].shape[-1], p["conv_b"].shape[-1]
    assert w_a == w_b, "the gate scratch is shared by both branches"
    hp_rows, hc_rows = POOL_HIST * nb, CONV_HIST * nb
    assert not carry or tm >= hc_rows
    n_groups = len(POOL_WINDOWS)
    attn, final = kv is not None, final_norm is not None

    args = [x, hp, hc, p["norm_mix"], p["w_in_even"], p["pool_w"], p["pool_scale"], p["conv_w"], p["conv_b"],
            p["ln_g"], p["ln_b"], p["w_out_even"]]
    hist_spec = (lambda r, c: _const_spec((r, c))) if carry else _row_spec
    specs = [_row_spec(tm, d), hist_spec(hp_rows, w_a), hist_spec(hc_rows, w_b),
             _const_spec((1, d), l), _const_spec((d, 2 * w_a + 3 * w_b), e),
             _const_spec((n_groups, w_a // n_groups, w_a // n_groups), e), _const_spec((1, w_a), e),
             _const_spec((CONV_TAPS, w_b), e), _const_spec((1, w_b), e), _const_spec((1, w_b), e),
             _const_spec((1, w_b), e), _const_spec((w_a + w_b, d), e)]
    if attn:
        a, s = _attn_inputs(p, l, kv, nb)
        args, specs = args + a, specs + s
    if final:
        args, specs = args + [final_norm], specs + [_const_spec((1, d))]

    state_spec = (lambda r, c: pl.BlockSpec((r, c), lambda i: (0, 0))) if carry else _row_spec
    n_state = 1 if carry else rows // tm
    scratch = [pltpu.VMEM((tm, d), BF16), pltpu.VMEM((hp_rows + tm, w_a), F32), pltpu.VMEM((tm, w_b), F32),
               pltpu.VMEM((tm, w_a), F32), pltpu.VMEM((tm, w_b), F32), pltpu.VMEM((tm, w_a + w_b), BF16),
               pltpu.VMEM((tm, d), F32)]
    if carry:
        scratch.append(pltpu.VMEM((hc_rows, w_b), F32))
    if attn:
        scratch += _attn_scratch(tm, d)
    return pl.pallas_call(
        functools.partial(_even_kernel, nb=nb, tt=tt, start_pos=start_pos, carry=carry, attn=attn, final=final),
        grid=(rows // tm,),
        in_specs=specs,
        out_specs=[_row_spec(tm, d), state_spec(hp_rows, w_a), state_spec(hc_rows, w_b)],
        out_shape=[jax.ShapeDtypeStruct((rows, d), F32), jax.ShapeDtypeStruct((n_state * hp_rows, w_a), F32),
                   jax.ShapeDtypeStruct((n_state * hc_rows, w_b), F32)],
        scratch_shapes=scratch,
        compiler_params=_params(1),
        name=f"even_layer_{l}_{'carry' if carry else 'group'}",
    )(*args)


def _odd_layer(x, hs, p, o, l, *, nb, tt, carry, kv=None, final_norm=None):
    rows, d = x.shape
    tm = nb * tt
    w_c = p["short_w"].shape[-1]
    hs_rows = SHORT_HIST * nb
    assert not carry or tm >= hs_rows
    attn, final = kv is not None, final_norm is not None

    args = [x, hs, p["norm_mix"], p["w_in_odd"], p["short_w"], p["w_out_odd"]]
    hist_spec = (lambda r, c: _const_spec((r, c))) if carry else _row_spec
    specs = [_row_spec(tm, d), hist_spec(hs_rows, w_c), _const_spec((1, d), l), _const_spec((d, 4 * w_c), o),
             _const_spec((SHORT_TAPS, w_c), o), _const_spec((w_c, d), o)]
    if attn:
        a, s = _attn_inputs(p, l, kv, nb)
        args, specs = args + a, specs + s
    if final:
        args, specs = args + [final_norm], specs + [_const_spec((1, d))]

    state_spec = (lambda r, c: pl.BlockSpec((r, c), lambda i: (0, 0))) if carry else _row_spec
    n_state = 1 if carry else rows // tm
    scratch = [pltpu.VMEM((tm, d), BF16), pltpu.VMEM((tm, w_c), F32), pltpu.VMEM((tm, w_c), F32),
               pltpu.VMEM((tm, w_c), BF16), pltpu.VMEM((tm, d), F32)]
    if carry:
        scratch.append(pltpu.VMEM((hs_rows, w_c), F32))
    if attn:
        scratch += _attn_scratch(tm, d)
    return pl.pallas_call(
        functools.partial(_odd_kernel, nb=nb, tt=tt, carry=carry, attn=attn, final=final),
        grid=(rows // tm,),
        in_specs=specs,
        out_specs=[_row_spec(tm, d), state_spec(hs_rows, w_c)],
        out_shape=[jax.ShapeDtypeStruct((rows, d), F32), jax.ShapeDtypeStruct((n_state * hs_rows, w_c), F32)],
        scratch_shapes=scratch,
        compiler_params=_params(1),
        name=f"odd_layer_{l}_{'carry' if carry else 'group'}",
    )(*args)


def _xattn_sample(x, kr, vr, p, l, *, nbs, tm, final_norm=None):
    rows, d = x.shape
    n_chunks = nbs // XATTN_SEQS
    final = final_norm is not None
    cache_block = (XATTN_SEQS,) + kr.shape[2:]
    cache_spec = pl.BlockSpec((None,) + cache_block, lambda g, c: (l, g * n_chunks + c, 0, 0))
    args = [x, p["norm_x"], p["w_q"], kr, vr, p["w_o"]]
    specs = [pl.BlockSpec((tm, d), lambda g, c: (g, 0)), _const_spec((1, d), l), _const_spec((d, d), l),
             cache_spec, cache_spec, _const_spec((d, d), l)]
    if final:
        args, specs = args + [final_norm], specs + [_const_spec((1, d))]
    return pl.pallas_call(
        functools.partial(_xattn_sample_kernel, nbs=nbs, final=final),
        grid=(rows // tm, n_chunks),
        in_specs=specs,
        out_specs=pl.BlockSpec((tm, d), lambda g, c: (g, 0)),
        out_shape=jax.ShapeDtypeStruct((rows, d), F32),
        scratch_shapes=[pltpu.VMEM((tm, d), BF16)] + _attn_scratch(tm, d) + [pltpu.VMEM((tm, d), F32)],
        compiler_params=_params(2),
        name=f"xattn_sample_{l}",
    )(*args)


def _memory_kv(mem, p, n_seq, n_mem, d_head):
    d = mem.shape[-1]
    depth = p["w_k"].shape[0]
    chunks = d_head // LANES
    r_rows = n_mem * chunks * N_HEADS
    r_spec = pl.BlockSpec((None, None, r_rows, LANES), lambda l, b: (l, b, 0, 0))
    b_spec = pl.BlockSpec((None, None, N_HEADS, n_mem, d_head), lambda l, b: (l, b, 0, 0, 0))
    w_spec = pl.BlockSpec((None, d, d), lambda l, b: (l, 0, 0))
    r_shape = jax.ShapeDtypeStruct((depth, n_seq, r_rows, LANES), F32)
    b_shape = jax.ShapeDtypeStruct((depth, n_seq, N_HEADS, n_mem, d_head), BF16)
    return pl.pallas_call(
        _memory_kv_kernel,
        grid=(depth, n_seq),
        in_specs=[pl.BlockSpec((n_mem, d), lambda l, b: (b, 0)), pl.BlockSpec((None, 1, d), lambda l, b: (l, 0, 0)),
                  w_spec, w_spec],
        out_specs=[r_spec, r_spec, b_spec, b_spec],
        out_shape=[r_shape, r_shape, b_shape, b_shape],
        scratch_shapes=[pltpu.VMEM((n_mem, d), BF16)],
        compiler_params=_params(2),
        name="memory_kv",
    )(mem, p["norm_mem"], p["w_k"], p["w_v"])


def _to_time_major(x, nb):
    b, t, c = x.shape
    return x.reshape(b // nb, nb, t, c).transpose(0, 2, 1, 3).reshape(b * t, c)


def _from_time_major(y, b, t, nb):
    c = y.shape[-1]
    return y.reshape(b // nb, t, nb, c).transpose(0, 2, 1, 3).reshape(b, t, c)


def _cache_rows(c):
    *lead, n_mem, heads, d_head = c.shape
    chunks = d_head // LANES
    r = c.reshape(*lead, n_mem, heads, chunks, LANES)
    return jnp.swapaxes(r, -3, -2).reshape(*lead, n_mem * chunks * heads, LANES)


def _cache_from_rows(r, n_mem, heads, d_head):
    *lead, _, _ = r.shape
    chunks = d_head // LANES
    c = r.reshape(*lead, n_mem, chunks, heads, LANES)
    return jnp.swapaxes(c, -3, -2).reshape(*lead, n_mem, heads, d_head)


def kernel(x_prompt, x_sample, state_pool, state_conv, state_short, cache_mem_k, cache_mem_v, mem_prompt, norm_mix, w_in_even, pool_w, pool_scale, conv_w, conv_b, ln_g, ln_b, w_out_even, w_in_odd, short_w, w_out_odd, norm_x, norm_mem, w_q, w_k, w_v, w_o, final_norm):
    b_p, t_p, d = x_prompt.shape
    b_s, t_s, _ = x_sample.shape
    depth = norm_mix.shape[0]
    n_mem = mem_prompt.shape[1]
    heads, d_head = cache_mem_k.shape[-2:]
    assert heads == N_HEADS and d_head % LANES == 0
    assert d_head & (d_head - 1) == 0 and d_head.bit_length() % 2 == 1, "the score scale must be a power of two"

    nb_p, tt_p = b_p, TILE_ROWS // b_p
    nb_s, tt_s = TILE_ROWS // t_s, t_s
    for nb in (nb_p, nb_s):
        assert nb % SUBLANES == 0 and nb & (nb - 1) == 0
    assert t_p % tt_p == 0 and b_s % nb_s == 0 and nb_s % XATTN_SEQS == 0

    row = lambda a: a.reshape(a.shape[0], 1, a.shape[-1])
    p = {
        "norm_mix": row(norm_mix), "norm_x": row(norm_x), "norm_mem": row(norm_mem),
        "w_in_even": w_in_even.astype(BF16), "pool_w": pool_w.astype(BF16), "pool_scale": row(pool_scale),
        "conv_w": conv_w, "conv_b": row(conv_b), "ln_g": row(ln_g), "ln_b": row(ln_b),
        "w_out_even": w_out_even.astype(BF16),
        "w_in_odd": w_in_odd.astype(BF16), "short_w": short_w, "w_out_odd": w_out_odd.astype(BF16),
        "w_q": w_q.astype(BF16), "w_k": w_k.astype(BF16), "w_v": w_v.astype(BF16), "w_o": w_o.astype(BF16),
    }
    fnorm = final_norm.reshape(1, d)
    w_a, w_b, w_c = pool_scale.shape[-1], conv_b.shape[-1], short_w.shape[-1]

    kr, vr, kb, vb = _memory_kv(mem_prompt.reshape(b_p * n_mem, d), p, b_p, n_mem, d_head)
    mem_k_p = _cache_from_rows(kr, n_mem, heads, d_head)
    mem_v_p = _cache_from_rows(vr, n_mem, heads, d_head)

    xp = _to_time_major(x_prompt, nb_p)
    zp = jnp.zeros((POOL_HIST * nb_p, w_a), F32)
    zc = jnp.zeros((CONV_HIST * nb_p, w_b), F32)
    zs = jnp.zeros((SHORT_HIST * nb_p, w_c), F32)
    pool_p, conv_p, short_p = [], [], []
    for l in range(depth):
        fin = fnorm if l == depth - 1 else None
        if l % 2 == 0:
            xp, sp, sc = _even_layer(xp, zp, zc, p, l // 2, l, nb=nb_p, tt=tt_p, start_pos=0, carry=True,
                                     kv=(kb, vb), final_norm=fin)
            pool_p.append(_from_time_major(sp, b_p, POOL_HIST, nb_p))
            conv_p.append(_from_time_major(sc, b_p, CONV_HIST, nb_p))
        else:
            xp, ss = _odd_layer(xp, zs, p, l // 2, l, nb=nb_p, tt=tt_p, carry=True, kv=(kb, vb), final_norm=fin)
            short_p.append(_from_time_major(ss, b_p, SHORT_HIST, nb_p))
    y_prompt = _from_time_major(xp, b_p, t_p, nb_p)

    krs, vrs = _cache_rows(cache_mem_k), _cache_rows(cache_mem_v)
    xs = _to_time_major(x_sample, nb_s)
    pool_s, conv_s, short_s = [], [], []
    for l in range(depth):
        fin = fnorm if l == depth - 1 else None
        if l % 2 == 0:
            e = l // 2
            xs, sp, sc = _even_layer(xs, _to_time_major(state_pool[e], nb_s), _to_time_major(state_conv[e], nb_s),
                                     p, e, l, nb=nb_s, tt=tt_s, start_pos=PAST_LEN, carry=False)
            pool_s.append(_from_time_major(sp, b_s, POOL_HIST, nb_s))
            conv_s.append(_from_time_major(sc, b_s, CONV_HIST, nb_s))
        else:
            o = l // 2
            xs, ss = _odd_layer(xs, _to_time_major(state_short[o], nb_s), p, o, l, nb=nb_s, tt=tt_s, carry=False)
            short_s.append(_from_time_major(ss, b_s, SHORT_HIST, nb_s))
        xs = _xattn_sample(xs, krs, vrs, p, l, nbs=nb_s, tm=nb_s * tt_s, final_norm=fin)
    y_sample = _from_time_major(xs, b_s, t_s, nb_s)

    return (y_prompt, y_sample, jnp.stack(pool_p), jnp.stack(conv_p), jnp.stack(short_p), mem_k_p, mem_v_p,
            jnp.stack(pool_s), jnp.stack(conv_s), jnp.stack(short_s))
```

```python
import functools

import jax
import jax.numpy as jnp
from jax import lax
from jax.experimental import pallas as pl
from jax.experimental.pallas import tpu as pltpu

F32 = jnp.float32
BF16 = jnp.bfloat16

EPS = 1e-6
POOL_WINDOWS = (2, 4, 8, 16)
POOL_HIST = max(POOL_WINDOWS) - 1
CONV_TAPS = 31
CONV_HIST = CONV_TAPS - 1
SHORT_TAPS = 3
SHORT_HIST = SHORT_TAPS - 1
N_HEADS = 4
PAST_LEN = 16384

LANES = 128
SUBLANES = 8
VMEM_LIMIT_BYTES = 56 * 1024 * 1024

TILE_ROWS = 256
NORM_ROWS = 32
XATTN_SEQS = 8


def _mm(a, b):
    return jnp.dot(a, b, preferred_element_type=F32)


def _mm_nt(a, b):
    return lax.dot_general(a, b, (((1,), (1,)), ((), ())), preferred_element_type=F32)


def _silu(x):
    return x * jax.nn.sigmoid(x)


def _rmsnorm(x, g):
    ms = jnp.mean(x * x, axis=-1, keepdims=True)
    return x * lax.rsqrt(ms + EPS) * g


def _softmax(s):
    e = jnp.exp(s - jnp.max(s, axis=-1, keepdims=True))
    return e / jnp.sum(e, axis=-1, keepdims=True)


def _norm_rows(dst_ref, src_ref, g_ref, dtype):
    rows = src_ref.shape[0]
    for r0 in range(0, rows, NORM_ROWS):
        rs = slice(r0, r0 + NORM_ROWS)
        dst_ref[rs, :] = _rmsnorm(src_ref[rs, :], g_ref[...]).astype(dtype)


def _window_rows(hist_ref, new_ref, start, n_rows, cols):
    hist_rows = hist_ref.shape[0]
    n_hist = min(max(hist_rows - start, 0), n_rows)
    parts = []
    if n_hist > 0:
        parts.append(hist_ref[start:start + n_hist, cols])
    if n_hist < n_rows:
        s = start + n_hist - hist_rows
        parts.append(new_ref[s:s + n_rows - n_hist, cols])
    return parts[0] if len(parts) == 1 else jnp.concatenate(parts, axis=0)


def _last_rows(dst_ref, hist_ref, new_ref):
    n, tm = dst_ref.shape[0], new_ref.shape[0]
    if tm >= n:
        dst_ref[...] = new_ref[tm - n:tm, :]
    else:
        dst_ref[0:n - tm, :] = hist_ref[tm:n, :]
        dst_ref[n - tm:n, :] = new_ref[...]


def _project_q(h_scr, wq_ref, q_scr, d_head):
    chunks = d_head // LANES
    for hd in range(N_HEADS):
        q = _mm(h_scr[...], wq_ref[:, hd * d_head:(hd + 1) * d_head]) * d_head ** -0.5
        for c in range(chunks):
            q_scr[hd * chunks + c] = q[:, c * LANES:(c + 1) * LANES]


def _attend_pairs(qg, k, v):
    s = jnp.einsum("pqd,pkd->pqk", qg, k, preferred_element_type=F32)
    return jnp.einsum("pqk,pkd->pqd", _softmax(s).astype(BF16), v, preferred_element_type=F32)


def _cross_attend(x1_scr, nx_ref, wq_ref, k_ref, v_ref, wo_ref, h_scr, q_scr, oh_scr, qg_scr, nb, tt):
    d_head = k_ref.shape[-1]
    chunks = d_head // LANES
    _norm_rows(h_scr, x1_scr, nx_ref, BF16)
    _project_q(h_scr, wq_ref, q_scr, d_head)
    for b in range(nb):
        rows = pl.ds(b, tt, stride=nb)
        for j in range(N_HEADS * chunks):
            qg_scr[b * N_HEADS + j // chunks, :, (j % chunks) * LANES:(j % chunks + 1) * LANES] = (
                q_scr[j, rows, :].astype(BF16))
    o = _attend_pairs(qg_scr[...], k_ref[...], v_ref[...])
    for b in range(nb):
        rows = pl.ds(b, tt, stride=nb)
        for j in range(N_HEADS * chunks):
            oh_scr[j, rows, :] = o[b * N_HEADS + j // chunks, :, (j % chunks) * LANES:(j % chunks + 1) * LANES]
    o_all = jnp.concatenate([oh_scr[j] for j in range(N_HEADS * chunks)], axis=1)
    return x1_scr[...] + _mm(o_all.astype(BF16), wo_ref[...])


def _finish(o_ref, out, x1_scr, fn_ref):
    if fn_ref is None:
        o_ref[...] = out
    else:
        x1_scr[...] = out
        _norm_rows(o_ref, x1_scr, fn_ref, F32)


def _even_kernel(*refs, nb, tt, start_pos, carry, attn, final):
    tm = nb * tt
    it = iter(refs)
    x_ref, hp_ref, hc_ref = next(it), next(it), next(it)
    nmix_ref, win_ref, poolw_ref, pscale_ref = next(it), next(it), next(it), next(it)
    convw_ref, convb_ref, lng_ref, lnb_ref, wout_ref = next(it), next(it), next(it), next(it), next(it)
    if attn:
        nx_ref, wq_ref, k_ref, v_ref, wo_ref = next(it), next(it), next(it), next(it), next(it)
    fn_ref = next(it) if final else None
    o_ref, newp_ref, newc_ref = next(it), next(it), next(it)
    h_scr, pool_scr, vnew_scr, t_scr, c_scr, ac_scr, x1_scr = (next(it) for _ in range(7))
    hc_scr = next(it) if carry else None
    if attn:
        q_scr, oh_scr, qg_scr = next(it), next(it), next(it)

    i = pl.program_id(0)
    n_steps = pl.num_programs(0)
    w_a = pscale_ref.shape[-1]
    w_b = convb_ref.shape[-1]
    hp_rows = POOL_HIST * nb
    pool_group = w_a // len(POOL_WINDOWS)

    if carry:
        @pl.when(i == 0)
        def _():
            pool_scr[0:hp_rows, :] = hp_ref[...]
            hc_scr[...] = hc_ref[...]
        hist_c = hc_scr
    else:
        pool_scr[0:hp_rows, :] = hp_ref[...]
        hist_c = hc_ref

    _norm_rows(h_scr, x_ref, nmix_ref, BF16)

    pool_scr[hp_rows:hp_rows + tm, :] = _mm(h_scr[...], win_ref[:, 0:w_a])
    t_scr[...] = _mm(h_scr[...], win_ref[:, w_a:2 * w_a])

    row = lax.broadcasted_iota(jnp.int32, (tm, LANES), 0)
    step = row >> (nb.bit_length() - 1)
    pos1 = step + (start_pos + 1 + (i * tt if carry else 0))
    for g, w in enumerate(POOL_WINDOWS):
        cnt = jnp.minimum(pos1, w).astype(F32)
        a_parts = []
        for c0 in range(g * pool_group, (g + 1) * pool_group, LANES):
            e = pool_scr[(POOL_HIST + 1 - w) * nb:hp_rows + tm, c0:c0 + LANES]
            s, sh = e, 1
            while sh < w:
                n = s.shape[0]
                s = s[sh * nb:, :] + s[:n - sh * nb, :]
                sh *= 2
            a_parts.append((s / cnt - e[(w - 1) * nb:, :]).astype(BF16))
        cs = slice(g * pool_group, (g + 1) * pool_group)
        pa = _mm(jnp.concatenate(a_parts, axis=1), poolw_ref[g])
        ac_scr[:, cs] = (pa * pscale_ref[:, cs] * _silu(t_scr[:, cs])).astype(BF16)

    bval = _mm(h_scr[...], win_ref[:, 2 * w_a:2 * w_a + w_b])
    bglu = _mm(h_scr[...], win_ref[:, 2 * w_a + w_b:2 * w_a + 2 * w_b])
    vnew_scr[...] = bval * jax.nn.sigmoid(bglu)
    t_scr[...] = _mm(h_scr[...], win_ref[:, 2 * w_a + 2 * w_b:2 * w_a + 3 * w_b])
    for c0 in range(0, w_b, LANES):
        cs = slice(c0, c0 + LANES)
        acc = jnp.broadcast_to(convb_ref[:, cs], (tm, LANES))
        for k in range(CONV_TAPS):
            acc = acc + _window_rows(hist_c, vnew_scr, k * nb, tm, cs) * convw_ref[k:k + 1, cs]
        c_scr[:, cs] = acc
    for r0 in range(0, tm, NORM_ROWS):
        rs = slice(r0, r0 + NORM_ROWS)
        c = c_scr[rs, :]
        d = c - jnp.mean(c, axis=-1, keepdims=True)
        var = jnp.mean(d * d, axis=-1, keepdims=True)
        y = d * lax.rsqrt(var + EPS) * lng_ref[...] + lnb_ref[...]
        ac_scr[rs, w_a:w_a + w_b] = (_silu(y) * _silu(t_scr[rs, :])).astype(BF16)

    x1 = x_ref[...] + _mm(ac_scr[...], wout_ref[...])

    if carry:
        _last_rows(hc_scr, hc_scr, vnew_scr)
        hp_new = pool_scr[tm:tm + hp_rows, :]
        pool_scr[0:hp_rows, :] = hp_new

        @pl.when(i == n_steps - 1)
        def _():
            newp_ref[...] = pool_scr[0:hp_rows, :]
            newc_ref[...] = hc_scr[...]
    else:
        newp_ref[...] = pool_scr[tm:tm + hp_rows, :]
        _last_rows(newc_ref, hc_ref, vnew_scr)

    if attn:
        x1_scr[...] = x1
        x1 = _cross_attend(x1_scr, nx_ref, wq_ref, k_ref, v_ref, wo_ref, h_scr, q_scr, oh_scr, qg_scr, nb, tt)
    _finish(o_ref, x1, x1_scr, fn_ref)


def _odd_kernel(*refs, nb, tt, carry, attn, final):
    tm = nb * tt
    it = iter(refs)
    x_ref, hs_ref = next(it), next(it)
    nmix_ref, win_ref, shortw_ref, wout_ref = next(it), next(it), next(it), next(it)
    if attn:
        nx_ref, wq_ref, k_ref, v_ref, wo_ref = next(it), next(it), next(it), next(it), next(it)
    fn_ref = next(it) if final else None
    o_ref, news_ref = next(it), next(it)
    h_scr, unew_scr, gate_scr, y_scr, x1_scr = (next(it) for _ in range(5))
    hs_scr = next(it) if carry else None
    if attn:
        q_scr, oh_scr, qg_scr = next(it), next(it), next(it)

    i = pl.program_id(0)
    n_steps = pl.num_programs(0)
    w_c = shortw_ref.shape[-1]
    d_model = x_ref.shape[-1]

    if carry:
        @pl.when(i == 0)
        def _():
            hs_scr[...] = hs_ref[...]
        hist = hs_scr
    else:
        hist = hs_ref

    _norm_rows(h_scr, x_ref, nmix_ref, BF16)

    for c0 in range(0, w_c, d_model):
        cs = slice(c0, c0 + d_model)
        gc = _mm(h_scr[...], win_ref[:, w_c + c0:w_c + c0 + d_model])
        xv = _mm(h_scr[...], win_ref[:, 2 * w_c + c0:2 * w_c + c0 + d_model])
        unew_scr[:, cs] = gc * xv
    for c0 in range(0, w_c, d_model):
        cs = slice(c0, c0 + d_model)
        gb = _mm(h_scr[...], win_ref[:, c0:c0 + d_model])
        gg = _mm(h_scr[...], win_ref[:, 3 * w_c + c0:3 * w_c + c0 + d_model])
        gate_scr[:, cs] = gb * _silu(gg)
    for c0 in range(0, w_c, LANES):
        cs = slice(c0, c0 + LANES)
        acc = _window_rows(hist, unew_scr, 0, tm, cs) * shortw_ref[0:1, cs]
        for k in range(1, SHORT_TAPS):
            acc = acc + _window_rows(hist, unew_scr, k * nb, tm, cs) * shortw_ref[k:k + 1, cs]
        y_scr[:, cs] = (gate_scr[:, cs] * acc).astype(BF16)

    x1 = x_ref[...] + _mm(y_scr[...], wout_ref[...])

    if carry:
        _last_rows(hs_scr, hs_scr, unew_scr)

        @pl.when(i == n_steps - 1)
        def _():
            news_ref[...] = hs_scr[...]
    else:
        _last_rows(news_ref, hs_ref, unew_scr)

    if attn:
        x1_scr[...] = x1
        x1 = _cross_attend(x1_scr, nx_ref, wq_ref, k_ref, v_ref, wo_ref, h_scr, q_scr, oh_scr, qg_scr, nb, tt)
    _finish(o_ref, x1, x1_scr, fn_ref)


def _xattn_sample_kernel(*refs, nbs, final):
    it = iter(refs)
    x_ref, nx_ref, wq_ref, k_ref, v_ref, wo_ref = (next(it) for _ in range(6))
    fn_ref = next(it) if final else None
    o_ref = next(it)
    h_scr, q_scr, oh_scr, x1_scr, qg_scr, kg_scr, vg_scr = (next(it) for _ in range(7))

    c = pl.program_id(1)
    tt = x_ref.shape[0] // nbs
    seqs = k_ref.shape[0]
    d_head = wq_ref.shape[-1] // N_HEADS
    chunks = d_head // LANES
    per_mem = chunks * N_HEADS
    n_mem = k_ref.shape[1] // per_mem

    @pl.when(c == 0)
    def _():
        _norm_rows(h_scr, x_ref, nx_ref, BF16)
        _project_q(h_scr, wq_ref, q_scr, d_head)

    for bl in range(seqs):
        rows = pl.ds(c * seqs + bl, tt, stride=nbs)
        for j in range(N_HEADS * chunks):
            hd, cs = j // chunks, slice((j % chunks) * LANES, (j % chunks + 1) * LANES)
            cache_rows = pl.ds((j % chunks) * N_HEADS + hd, n_mem, stride=per_mem)
            qg_scr[bl * N_HEADS + hd, :, cs] = q_scr[j, rows, :]
            kg_scr[bl * N_HEADS + hd, :, cs] = k_ref[bl, cache_rows, :].astype(BF16)
            vg_scr[bl * N_HEADS + hd, :, cs] = v_ref[bl, cache_rows, :].astype(BF16)
    o = _attend_pairs(qg_scr[...].astype(BF16), kg_scr[...], vg_scr[...])
    for bl in range(seqs):
        rows = pl.ds(c * seqs + bl, tt, stride=nbs)
        for j in range(N_HEADS * chunks):
            oh_scr[j, rows, :] = o[bl * N_HEADS + j // chunks, :, (j % chunks) * LANES:(j % chunks + 1) * LANES]

    @pl.when(c == pl.num_programs(1) - 1)
    def _():
        o_all = jnp.concatenate([oh_scr[j] for j in range(N_HEADS * chunks)], axis=1)
        out = x_ref[...] + _mm(o_all.astype(BF16), wo_ref[...])
        _finish(o_ref, out, x1_scr, fn_ref)


def _memory_kv_kernel(mem_ref, nmem_ref, wk_ref, wv_ref, kr_ref, vr_ref, kb_ref, vb_ref, h_scr):
    n_mem = mem_ref.shape[0]
    d_head = kb_ref.shape[-1]
    chunks = d_head // LANES
    per_mem = chunks * N_HEADS
    _norm_rows(h_scr, mem_ref, nmem_ref, BF16)
    for w_ref, r_ref, b_ref in ((wk_ref, kr_ref, kb_ref), (wv_ref, vr_ref, vb_ref)):
        for hd in range(N_HEADS):
            kh = _mm(h_scr[...], w_ref[:, hd * d_head:(hd + 1) * d_head])
            b_ref[hd] = kh.astype(BF16)
            for cc in range(chunks):
                r_ref[pl.ds(cc * N_HEADS + hd, n_mem, stride=per_mem), :] = kh[:, cc * LANES:(cc + 1) * LANES]


def _const_spec(shape, layer=None):
    if layer is None:
        return pl.BlockSpec(shape, lambda *_: (0,) * len(shape), pipeline_mode=pl.Buffered(1))
    return pl.BlockSpec((None,) + shape, lambda *_: (layer,) + (0,) * len(shape), pipeline_mode=pl.Buffered(1))


def _row_spec(rows, cols):
    return pl.BlockSpec((rows, cols), lambda i: (i, 0))


def _params(n_axes):
    return pltpu.CompilerParams(dimension_semantics=("arbitrary",) * n_axes, vmem_limit_bytes=VMEM_LIMIT_BYTES)


def _attn_inputs(p, l, kv, nb):
    kb, vb = kv
    d = p["w_q"].shape[-1]
    assert kb.shape[1] == nb * N_HEADS
    args = [p["norm_x"], p["w_q"], kb, vb, p["w_o"]]
    specs = [_const_spec((1, d), l), _const_spec((d, d), l), _const_spec(kb.shape[1:], l),
             _const_spec(vb.shape[1:], l), _const_spec((d, d), l)]
    return args, specs


def _attn_scratch(tm, d, n_pairs, tt, q_dtype):
    return [pltpu.VMEM((d // LANES, tm, LANES), F32), pltpu.VMEM((d // LANES, tm, LANES), F32),
            pltpu.VMEM((n_pairs, tt, d // N_HEADS), q_dtype)]


def _even_layer(x, hp, hc, p, e, l, *, nb, tt, start_pos, carry, kv=None, final_norm=None):
    rows, d = x.shape
    tm = nb * tt
    w_a, w_b = p["pool_scale"].shape[-1], p["conv_b"].shape[-1]
    assert w_a == w_b, "the gate scratch is shared by both branches"
    hp_rows, hc_rows = POOL_HIST * nb, CONV_HIST * nb
    assert not carry or tm >= hc_rows
    n_groups = len(POOL_WINDOWS)
    attn, final = kv is not None, final_norm is not None

    args = [x, hp, hc, p["norm_mix"], p["w_in_even"], p["pool_w"], p["pool_scale"], p["conv_w"], p["conv_b"],
            p["ln_g"], p["ln_b"], p["w_out_even"]]
    hist_spec = (lambda r, c: _const_spec((r, c))) if carry else _row_spec
    specs = [_row_spec(tm, d), hist_spec(hp_rows, w_a), hist_spec(hc_rows, w_b),
             _const_spec((1, d), l), _const_spec((d, 2 * w_a + 3 * w_b), e),
             _const_spec((n_groups, w_a // n_groups, w_a // n_groups), e), _const_spec((1, w_a), e),
             _const_spec((CONV_TAPS, w_b), e), _const_spec((1, w_b), e), _const_spec((1, w_b), e),
             _const_spec((1, w_b), e), _const_spec((w_a + w_b, d), e)]
    if attn:
        a, s = _attn_inputs(p, l, kv, nb)
        args, specs = args + a, specs + s
    if final:
        args, specs = args + [final_norm], specs + [_const_spec((1, d))]

    state_spec = (lambda r, c: pl.BlockSpec((r, c), lambda i: (0, 0))) if carry else _row_spec
    n_state = 1 if carry else rows // tm
    scratch = [pltpu.VMEM((tm, d), BF16), pltpu.VMEM((hp_rows + tm, w_a), F32), pltpu.VMEM((tm, w_b), F32),
               pltpu.VMEM((tm, w_a), F32), pltpu.VMEM((tm, w_b), F32), pltpu.VMEM((tm, w_a + w_b), BF16),
               pltpu.VMEM((tm, d), F32)]
    if carry:
        scratch.append(pltpu.VMEM((hc_rows, w_b), F32))
    if attn:
        scratch += _attn_scratch(tm, d, nb * N_HEADS, tt, BF16)
    return pl.pallas_call(
        functools.partial(_even_kernel, nb=nb, tt=tt, start_pos=start_pos, carry=carry, attn=attn, final=final),
        grid=(rows // tm,),
        in_specs=specs,
        out_specs=[_row_spec(tm, d), state_spec(hp_rows, w_a), state_spec(hc_rows, w_b)],
        out_shape=[jax.ShapeDtypeStruct((rows, d), F32), jax.ShapeDtypeStruct((n_state * hp_rows, w_a), F32),
                   jax.ShapeDtypeStruct((n_state * hc_rows, w_b), F32)],
        scratch_shapes=scratch,
        compiler_params=_params(1),
        name=f"even_layer_{l}_{'carry' if carry else 'group'}",
    )(*args)


def _odd_layer(x, hs, p, o, l, *, nb, tt, carry, kv=None, final_norm=None):
    rows, d = x.shape
    tm = nb * tt
    w_c = p["short_w"].shape[-1]
    hs_rows = SHORT_HIST * nb
    assert not carry or tm >= hs_rows
    attn, final = kv is not None, final_norm is not None

    args = [x, hs, p["norm_mix"], p["w_in_odd"], p["short_w"], p["w_out_odd"]]
    hist_spec = (lambda r, c: _const_spec((r, c))) if carry else _row_spec
    specs = [_row_spec(tm, d), hist_spec(hs_rows, w_c), _const_spec((1, d), l), _const_spec((d, 4 * w_c), o),
             _const_spec((SHORT_TAPS, w_c), o), _const_spec((w_c, d), o)]
    if attn:
        a, s = _attn_inputs(p, l, kv, nb)
        args, specs = args + a, specs + s
    if final:
        args, specs = args + [final_norm], specs + [_const_spec((1, d))]

    state_spec = (lambda r, c: pl.BlockSpec((r, c), lambda i: (0, 0))) if carry else _row_spec
    n_state = 1 if carry else rows // tm
    scratch = [pltpu.VMEM((tm, d), BF16), pltpu.VMEM((tm, w_c), F32), pltpu.VMEM((tm, w_c), F32),
               pltpu.VMEM((tm, w_c), BF16), pltpu.VMEM((tm, d), F32)]
    if carry:
        scratch.append(pltpu.VMEM((hs_rows, w_c), F32))
    if attn:
        scratch += _attn_scratch(tm, d, nb * N_HEADS, tt, BF16)
    return pl.pallas_call(
        functools.partial(_odd_kernel, nb=nb, tt=tt, carry=carry, attn=attn, final=final),
        grid=(rows // tm,),
        in_specs=specs,
        out_specs=[_row_spec(tm, d), state_spec(hs_rows, w_c)],
        out_shape=[jax.ShapeDtypeStruct((rows, d), F32), jax.ShapeDtypeStruct((n_state * hs_rows, w_c), F32)],
        scratch_shapes=scratch,
        compiler_params=_params(1),
        name=f"odd_layer_{l}_{'carry' if carry else 'group'}",
    )(*args)


def _xattn_sample(x, kr, vr, p, l, *, nbs, tm, final_norm=None):
    rows, d = x.shape
    n_chunks = nbs // XATTN_SEQS
    final = final_norm is not None
    d_head = d // N_HEADS
    n_pairs = XATTN_SEQS * N_HEADS
    n_mem = kr.shape[2] * LANES // d
    cache_block = (XATTN_SEQS,) + kr.shape[2:]
    cache_spec = pl.BlockSpec((None,) + cache_block, lambda g, c: (l, g * n_chunks + c, 0, 0))
    args = [x, p["norm_x"], p["w_q"], kr, vr, p["w_o"]]
    specs = [pl.BlockSpec((tm, d), lambda g, c: (g, 0)), _const_spec((1, d), l), _const_spec((d, d), l),
             cache_spec, cache_spec, _const_spec((d, d), l)]
    if final:
        args, specs = args + [final_norm], specs + [_const_spec((1, d))]
    return pl.pallas_call(
        functools.partial(_xattn_sample_kernel, nbs=nbs, final=final),
        grid=(rows // tm, n_chunks),
        in_specs=specs,
        out_specs=pl.BlockSpec((tm, d), lambda g, c: (g, 0)),
        out_shape=jax.ShapeDtypeStruct((rows, d), F32),
        scratch_shapes=[pltpu.VMEM((tm, d), BF16), pltpu.VMEM((d // LANES, tm, LANES), F32),
                        pltpu.VMEM((d // LANES, tm, LANES), F32), pltpu.VMEM((tm, d), F32),
                        pltpu.VMEM((n_pairs, tm // nbs, d_head), F32), pltpu.VMEM((n_pairs, n_mem, d_head), BF16),
                        pltpu.VMEM((n_pairs, n_mem, d_head), BF16)],
        compiler_params=_params(2),
        name=f"xattn_sample_{l}",
    )(*args)


def _memory_kv(mem, p, n_seq, n_mem, d_head):
    d = mem.shape[-1]
    depth = p["w_k"].shape[0]
    chunks = d_head // LANES
    r_rows = n_mem * chunks * N_HEADS
    r_spec = pl.BlockSpec((None, None, r_rows, LANES), lambda l, b: (l, b, 0, 0))
    b_spec = pl.BlockSpec((None, None, N_HEADS, n_mem, d_head), lambda l, b: (l, b, 0, 0, 0))
    w_spec = pl.BlockSpec((None, d, d), lambda l, b: (l, 0, 0))
    r_shape = jax.ShapeDtypeStruct((depth, n_seq, r_rows, LANES), F32)
    b_shape = jax.ShapeDtypeStruct((depth, n_seq, N_HEADS, n_mem, d_head), BF16)
    return pl.pallas_call(
        _memory_kv_kernel,
        grid=(depth, n_seq),
        in_specs=[pl.BlockSpec((n_mem, d), lambda l, b: (b, 0)), pl.BlockSpec((None, 1, d), lambda l, b: (l, 0, 0)),
                  w_spec, w_spec],
        out_specs=[r_spec, r_spec, b_spec, b_spec],
        out_shape=[r_shape, r_shape, b_shape, b_shape],
        scratch_shapes=[pltpu.VMEM((n_mem, d), BF16)],
        compiler_params=_params(2),
        name="memory_kv",
    )(mem, p["norm_mem"], p["w_k"], p["w_v"])


def _to_time_major(x, nb):
    b, t, c = x.shape
    return x.reshape(b // nb, nb, t, c).transpose(0, 2, 1, 3).reshape(b * t, c)


def _from_time_major(y, b, t, nb):
    c = y.shape[-1]
    return y.reshape(b // nb, t, nb, c).transpose(0, 2, 1, 3).reshape(b, t, c)


def _cache_rows(c):
    *lead, n_mem, heads, d_head = c.shape
    chunks = d_head // LANES
    r = c.reshape(*lead, n_mem, heads, chunks, LANES)
    return jnp.swapaxes(r, -3, -2).reshape(*lead, n_mem * chunks * heads, LANES)


def _cache_from_rows(r, n_mem, heads, d_head):
    *lead, _, _ = r.shape
    chunks = d_head // LANES
    c = r.reshape(*lead, n_mem, chunks, heads, LANES)
    return jnp.swapaxes(c, -3, -2).reshape(*lead, n_mem, heads, d_head)


def kernel(x_prompt, x_sample, state_pool, state_conv, state_short, cache_mem_k, cache_mem_v, mem_prompt, norm_mix, w_in_even, pool_w, pool_scale, conv_w, conv_b, ln_g, ln_b, w_out_even, w_in_odd, short_w, w_out_odd, norm_x, norm_mem, w_q, w_k, w_v, w_o, final_norm):
    b_p, t_p, d = x_prompt.shape
    b_s, t_s, _ = x_sample.shape
    depth = norm_mix.shape[0]
    n_mem = mem_prompt.shape[1]
    heads, d_head = cache_mem_k.shape[-2:]
    assert heads == N_HEADS and d_head % LANES == 0
    assert d_head & (d_head - 1) == 0 and d_head.bit_length() % 2 == 1, "the score scale must be a power of two"

    nb_p, tt_p = b_p, TILE_ROWS // b_p
    nb_s, tt_s = TILE_ROWS // t_s, t_s
    for nb in (nb_p, nb_s):
        assert nb % SUBLANES == 0 and nb & (nb - 1) == 0
    assert t_p % tt_p == 0 and b_s % nb_s == 0 and nb_s % XATTN_SEQS == 0

    row = lambda a: a.reshape(a.shape[0], 1, a.shape[-1])
    p = {
        "norm_mix": row(norm_mix), "norm_x": row(norm_x), "norm_mem": row(norm_mem),
        "w_in_even": w_in_even.astype(BF16), "pool_w": pool_w.astype(BF16), "pool_scale": row(pool_scale),
        "conv_w": conv_w, "conv_b": row(conv_b), "ln_g": row(ln_g), "ln_b": row(ln_b),
        "w_out_even": w_out_even.astype(BF16),
        "w_in_odd": w_in_odd.astype(BF16), "short_w": short_w, "w_out_odd": w_out_odd.astype(BF16),
        "w_q": w_q.astype(BF16), "w_k": w_k.astype(BF16), "w_v": w_v.astype(BF16), "w_o": w_o.astype(BF16),
    }
    fnorm = final_norm.reshape(1, d)
    w_a, w_b, w_c = pool_scale.shape[-1], conv_b.shape[-1], short_w.shape[-1]

    kr, vr, kb, vb = _memory_kv(mem_prompt.reshape(b_p * n_mem, d), p, b_p, n_mem, d_head)
    mem_k_p = _cache_from_rows(kr, n_mem, heads, d_head)
    mem_v_p = _cache_from_rows(vr, n_mem, heads, d_head)
    kb = kb.reshape(depth, b_p * heads, n_mem, d_head)
    vb = vb.reshape(depth, b_p * heads, n_mem, d_head)

    xp = _to_time_major(x_prompt, nb_p)
    zp = jnp.zeros((POOL_HIST * nb_p, w_a), F32)
    zc = jnp.zeros((CONV_HIST * nb_p, w_b), F32)
    zs = jnp.zeros((SHORT_HIST * nb_p, w_c), F32)
    pool_p, conv_p, short_p = [], [], []
    for l in range(depth):
        fin = fnorm if l == depth - 1 else None
        if l % 2 == 0:
            xp, sp, sc = _even_layer(xp, zp, zc, p, l // 2, l, nb=nb_p, tt=tt_p, start_pos=0, carry=True,
                                     kv=(kb, vb), final_norm=fin)
            pool_p.append(_from_time_major(sp, b_p, POOL_HIST, nb_p))
            conv_p.append(_from_time_major(sc, b_p, CONV_HIST, nb_p))
        else:
            xp, ss = _odd_layer(xp, zs, p, l // 2, l, nb=nb_p, tt=tt_p, carry=True, kv=(kb, vb), final_norm=fin)
            short_p.append(_from_time_major(ss, b_p, SHORT_HIST, nb_p))
    y_prompt = _from_time_major(xp, b_p, t_p, nb_p)

    krs, vrs = _cache_rows(cache_mem_k), _cache_rows(cache_mem_v)
    xs = _to_time_major(x_sample, nb_s)
    pool_s, conv_s, short_s = [], [], []
    for l in range(depth):
        fin = fnorm if l == depth - 1 else None
        if l % 2 == 0:
            e = l // 2
            xs, sp, sc = _even_layer(xs, _to_time_major(state_pool[e], nb_s), _to_time_major(state_conv[e], nb_s),
                                     p, e, l, nb=nb_s, tt=tt_s, start_pos=PAST_LEN, carry=False)
            pool_s.append(_from_time_major(sp, b_s, POOL_HIST, nb_s))
            conv_s.append(_from_time_major(sc, b_s, CONV_HIST, nb_s))
        else:
            o = l // 2
            xs, ss = _odd_layer(xs, _to_time_major(state_short[o], nb_s), p, o, l, nb=nb_s, tt=tt_s, carry=False)
            short_s.append(_from_time_major(ss, b_s, SHORT_HIST, nb_s))
        xs = _xattn_sample(xs, krs, vrs, p, l, nbs=nb_s, tm=nb_s * tt_s, final_norm=fin)
    y_sample = _from_time_major(xs, b_s, t_s, nb_s)

    return (y_prompt, y_sample, jnp.stack(pool_p), jnp.stack(conv_p), jnp.stack(short_p), mem_k_p, mem_v_p,
            jnp.stack(pool_s), jnp.stack(conv_s), jnp.stack(short_s))
```

```python
import functools

import jax
import jax.numpy as jnp
from jax import lax
from jax.experimental import pallas as pl
from jax.experimental.pallas import tpu as pltpu

F32 = jnp.float32
BF16 = jnp.bfloat16

EPS = 1e-6
POOL_WINDOWS = (2, 4, 8, 16)
POOL_HIST = max(POOL_WINDOWS) - 1
CONV_TAPS = 31
CONV_HIST = CONV_TAPS - 1
SHORT_TAPS = 3
SHORT_HIST = SHORT_TAPS - 1
N_HEADS = 4
PAST_LEN = 16384

LANES = 128
SUBLANES = 8
MXU_COLS = 256
VMEM_LIMIT_BYTES = 56 * 1024 * 1024

TILE_ROWS = 256
ATTN_TILE_ROWS = 512
NORM_ROWS = 32
CONV_ROWS = 128
CONV_TAP_GROUP = 8
XATTN_SEQS = 8


def _mm(a, b):
    return jnp.dot(a, b, preferred_element_type=F32)


def _silu(x):
    return x * jax.nn.sigmoid(x)


def _rmsnorm(x, g):
    ms = jnp.mean(x * x, axis=-1, keepdims=True)
    return x * lax.rsqrt(ms + EPS) * g


def _softmax(s):
    e = jnp.exp(s - jnp.max(s, axis=-1, keepdims=True))
    return e / jnp.sum(e, axis=-1, keepdims=True)


def _norm_rows(dst_ref, src_ref, g_ref, dtype):
    rows = src_ref.shape[0]
    for r0 in range(0, rows, NORM_ROWS):
        rs = slice(r0, r0 + NORM_ROWS)
        dst_ref[rs, :] = _rmsnorm(src_ref[rs, :], g_ref[...]).astype(dtype)


def _window_rows(hist_ref, new_ref, start, n_rows, cols):
    hist_rows = hist_ref.shape[0]
    n_hist = min(max(hist_rows - start, 0), n_rows)
    parts = []
    if n_hist > 0:
        parts.append(hist_ref[start:start + n_hist, cols])
    if n_hist < n_rows:
        s = start + n_hist - hist_rows
        parts.append(new_ref[s:s + n_rows - n_hist, cols])
    return parts[0] if len(parts) == 1 else jnp.concatenate(parts, axis=0)


def _last_rows(dst_ref, hist_ref, new_ref):
    n, tm = dst_ref.shape[0], new_ref.shape[0]
    if tm >= n:
        dst_ref[...] = new_ref[tm - n:tm, :]
    else:
        dst_ref[0:n - tm, :] = hist_ref[tm:n, :]
        dst_ref[n - tm:n, :] = new_ref[...]


def _exact_zero_of(x):
    bits = lax.bitcast_convert_type(x, jnp.uint32)
    sixteen = jnp.full(bits.shape, 16, jnp.uint32)
    bits = lax.shift_right_logical(lax.shift_right_logical(bits, sixteen), sixteen)
    return lax.bitcast_convert_type(bits, F32)


def _lane_chunks(width):
    return [slice(c0, c0 + LANES) for c0 in range(0, width, LANES)]


def _even_kernel(*refs, nb, tt, start_pos, carry, batch_major_in):
    tm = nb * tt
    it = iter(refs)
    x_ref, hp_ref, hc_ref, nmix_ref, win_ref, poolw_ref, pscale_ref = (next(it) for _ in range(7))
    convw_ref, convb_ref, lng_ref, lnb_ref, wout_ref = (next(it) for _ in range(5))
    o_ref, newp_ref, newc_ref = next(it), next(it), next(it)
    h_scr, pool_scr, vnew_scr, ag_scr, bg_scr, c_scr, ac_scr, wtile_scr = (next(it) for _ in range(8))
    hc_scr = next(it) if carry else None
    if batch_major_in:
        xc_scr, xt_scr = next(it), next(it)

    i = pl.program_id(0)
    n_steps = pl.num_programs(0)
    w_a = pscale_ref.shape[-1]
    w_b = convb_ref.shape[-1]
    hp_rows = POOL_HIST * nb
    pool_group = w_a // len(POOL_WINDOWS)

    if carry:
        @pl.when(i == 0)
        def _():
            pool_scr[0:hp_rows, :] = hp_ref[...]
            hc_scr[...] = hc_ref[...]
        hist_c = hc_scr
    else:
        pool_scr[0:hp_rows, :] = hp_ref[...]
        hist_c = hc_ref

    if batch_major_in:
        for b in range(nb):
            for j, cs in enumerate(_lane_chunks(x_ref.shape[-1])):
                xc_scr[j, pl.ds(b, tt, stride=nb), :] = x_ref[b, :, cs]
        for j, cs in enumerate(_lane_chunks(x_ref.shape[-1])):
            xt_scr[:, cs] = xc_scr[j]
        x_src = xt_scr
    else:
        x_src = x_ref

    _norm_rows(h_scr, x_src, nmix_ref, BF16)

    bval = _mm(h_scr[...], win_ref[:, 2 * w_a:2 * w_a + w_b])
    bglu = _mm(h_scr[...], win_ref[:, 2 * w_a + w_b:2 * w_a + 2 * w_b])
    vnew_scr[...] = bval * jax.nn.sigmoid(bglu)

    def project(dst_ref, dst_rows, col0, c0):
        def run():
            dst_ref[dst_rows, c0:c0 + MXU_COLS] = _mm(h_scr[...], win_ref[:, col0 + c0:col0 + c0 + MXU_COLS])
        return run
    pieces = ([project(pool_scr, slice(hp_rows, hp_rows + tm), 0, c0) for c0 in range(0, w_a, MXU_COLS)]
              + [project(ag_scr, slice(None), w_a, c0) for c0 in range(0, w_a, MXU_COLS)]
              + [project(bg_scr, slice(None), 2 * w_a + 2 * w_b, c0) for c0 in range(0, w_b, MXU_COLS)])
    conv_chunks = _lane_chunks(w_b)
    for k in range(CONV_TAPS):
        wtile_scr[k * SUBLANES:(k + 1) * SUBLANES, :] = jnp.broadcast_to(convw_ref[k:k + 1, :], (SUBLANES, w_b))

    def conv_chunk(n, cs):
        order = None
        for r0 in range(0, tm, CONV_ROWS):
            bias = jnp.broadcast_to(convb_ref[:, cs], (SUBLANES, LANES))
            if order is not None:
                bias = bias + order
            acc = jnp.concatenate([bias] * (CONV_ROWS // SUBLANES), axis=0)
            for k0 in range(0, CONV_TAPS, CONV_TAP_GROUP):
                taps = range(k0, min(k0 + CONV_TAP_GROUP, CONV_TAPS))
                win = _window_rows(hist_c, vnew_scr, r0 + k0 * nb, (len(taps) - 1) * nb + CONV_ROWS, cs)
                for k in taps:
                    wk = wtile_scr[k * SUBLANES:(k + 1) * SUBLANES, cs]
                    acc = acc + (win[(k - k0) * nb:(k - k0) * nb + CONV_ROWS, :]
                                 * jnp.concatenate([wk] * (CONV_ROWS // SUBLANES), axis=0))
            c_scr[r0:r0 + CONV_ROWS, cs] = acc
            order = _exact_zero_of(acc[0:SUBLANES, :])
        for run in pieces[n * len(pieces) // len(conv_chunks):(n + 1) * len(pieces) // len(conv_chunks)]:
            run()

    for n, cs in enumerate(conv_chunks):
        pl.when(i >= 0)(functools.partial(conv_chunk, n, cs))

    row = lax.broadcasted_iota(jnp.int32, (tm, LANES), 0)
    step = row >> (nb.bit_length() - 1)
    pos1 = step + (start_pos + 1 + (i * tt if carry else 0))
    for g, w in enumerate(POOL_WINDOWS):
        cnt = jnp.minimum(pos1, w).astype(F32)
        a_parts = []
        for c0 in range(g * pool_group, (g + 1) * pool_group, LANES):
            e = pool_scr[(POOL_HIST + 1 - w) * nb:hp_rows + tm, c0:c0 + LANES]
            s, sh = e, 1
            while sh < w:
                n = s.shape[0]
                s = s[sh * nb:, :] + s[:n - sh * nb, :]
                sh *= 2
            a_parts.append((s / cnt - e[(w - 1) * nb:, :]).astype(BF16))
        cs = slice(g * pool_group, (g + 1) * pool_group)
        pa = _mm(jnp.concatenate(a_parts, axis=1), poolw_ref[g])
        ac_scr[:, cs] = (pa * pscale_ref[:, cs] * _silu(ag_scr[:, cs])).astype(BF16)

    for r0 in range(0, tm, NORM_ROWS):
        rs = slice(r0, r0 + NORM_ROWS)
        c = c_scr[rs, :]
        d = c - jnp.mean(c, axis=-1, keepdims=True)
        var = jnp.mean(d * d, axis=-1, keepdims=True)
        y = d * lax.rsqrt(var + EPS) * lng_ref[...] + lnb_ref[...]
        ac_scr[rs, w_a:w_a + w_b] = (_silu(y) * _silu(bg_scr[rs, :])).astype(BF16)

    o_ref[...] = x_src[...] + _mm(ac_scr[...], wout_ref[...])

    if carry:
        _last_rows(hc_scr, hc_scr, vnew_scr)
        hp_new = pool_scr[tm:tm + hp_rows, :]
        pool_scr[0:hp_rows, :] = hp_new

        @pl.when(i == n_steps - 1)
        def _():
            newp_ref[...] = pool_scr[0:hp_rows, :]
            newc_ref[...] = hc_scr[...]
    else:
        newp_ref[...] = pool_scr[tm:tm + hp_rows, :]
        _last_rows(newc_ref, hc_ref, vnew_scr)


def _odd_kernel(*refs, nb, tt, carry):
    tm = nb * tt
    it = iter(refs)
    x_ref, hs_ref, nmix_ref, win_ref, shortw_ref, wout_ref = (next(it) for _ in range(6))
    o_ref, news_ref = next(it), next(it)
    h_scr, unew_scr, gate_scr, y_scr = (next(it) for _ in range(4))
    hs_scr = next(it) if carry else None

    i = pl.program_id(0)
    n_steps = pl.num_programs(0)
    w_c = shortw_ref.shape[-1]
    d_model = x_ref.shape[-1]

    if carry:
        @pl.when(i == 0)
        def _():
            hs_scr[...] = hs_ref[...]
        hist = hs_scr
    else:
        hist = hs_ref

    _norm_rows(h_scr, x_ref, nmix_ref, BF16)

    for c0 in range(0, w_c, d_model):
        cs = slice(c0, c0 + d_model)
        gc = _mm(h_scr[...], win_ref[:, w_c + c0:w_c + c0 + d_model])
        xv = _mm(h_scr[...], win_ref[:, 2 * w_c + c0:2 * w_c + c0 + d_model])
        unew_scr[:, cs] = gc * xv
    for c0 in range(0, w_c, d_model):
        cs = slice(c0, c0 + d_model)
        gb = _mm(h_scr[...], win_ref[:, c0:c0 + d_model])
        gg = _mm(h_scr[...], win_ref[:, 3 * w_c + c0:3 * w_c + c0 + d_model])
        gate_scr[:, cs] = gb * _silu(gg)
    for cs in _lane_chunks(w_c):
        acc = _window_rows(hist, unew_scr, 0, tm, cs) * shortw_ref[0:1, cs]
        for k in range(1, SHORT_TAPS):
            acc = acc + _window_rows(hist, unew_scr, k * nb, tm, cs) * shortw_ref[k:k + 1, cs]
        y_scr[:, cs] = (gate_scr[:, cs] * acc).astype(BF16)

    o_ref[...] = x_ref[...] + _mm(y_scr[...], wout_ref[...])

    if carry:
        _last_rows(hs_scr, hs_scr, unew_scr)

        @pl.when(i == n_steps - 1)
        def _():
            news_ref[...] = hs_scr[...]
    else:
        _last_rows(news_ref, hs_ref, unew_scr)


def _project_q(h_scr, wq_ref, q_scr, d_head):
    chunks = d_head // LANES
    for hd in range(N_HEADS):
        q = _mm(h_scr[...], wq_ref[:, hd * d_head:(hd + 1) * d_head]) * d_head ** -0.5
        for c in range(chunks):
            q_scr[hd * chunks + c] = q[:, c * LANES:(c + 1) * LANES]


def _pair_slot(seq, j, chunks):
    return seq * N_HEADS + j // chunks, slice((j % chunks) * LANES, (j % chunks + 1) * LANES)


def _xattn_prompt_kernel(*refs, nb, tt, final, batch_major_out):
    it = iter(refs)
    x_ref, nx_ref, wq_ref, kt_ref, v_ref, wo_ref = (next(it) for _ in range(6))
    fn_ref = next(it) if final else None
    o_ref = next(it)
    h_scr, q_scr, oh_scr, qg_scr, y_scr = (next(it) for _ in range(5))
    d_head = v_ref.shape[-1]
    chunks = d_head // LANES
    n_chunks = N_HEADS * chunks

    _norm_rows(h_scr, x_ref, nx_ref, BF16)
    _project_q(h_scr, wq_ref, q_scr, d_head)
    for b in range(nb):
        for j in range(n_chunks):
            pair, cs = _pair_slot(b, j, chunks)
            qg_scr[pair, :, cs] = q_scr[j, pl.ds(b, tt, stride=nb), :].astype(BF16)
    s = jnp.einsum("pqd,pdk->pqk", qg_scr[...], kt_ref[...], preferred_element_type=F32)
    o = jnp.einsum("pqk,pkd->pqd", _softmax(s).astype(BF16), v_ref[...], preferred_element_type=F32)
    for b in range(nb):
        for j in range(n_chunks):
            pair, cs = _pair_slot(b, j, chunks)
            oh_scr[j, pl.ds(b, tt, stride=nb), :] = o[pair, :, cs]
    o_all = jnp.concatenate([oh_scr[j] for j in range(n_chunks)], axis=1)
    out = x_ref[...] + _mm(o_all.astype(BF16), wo_ref[...])

    if not (final or batch_major_out):
        o_ref[...] = out
        return
    y_scr[...] = out
    if final:
        _norm_rows(y_scr, y_scr, fn_ref, F32)
    if not batch_major_out:
        o_ref[...] = y_scr[...]
        return
    for j, cs in enumerate(_lane_chunks(y_scr.shape[-1])):
        q_scr[j] = y_scr[:, cs]
    for b in range(nb):
        for j, cs in enumerate(_lane_chunks(y_scr.shape[-1])):
            o_ref[b, :, cs] = q_scr[j, pl.ds(b, tt, stride=nb), :]


def _xattn_sample_kernel(*refs, nbs, final):
    it = iter(refs)
    x_ref, nx_ref, wq_ref, k_ref, v_ref, wo_ref = (next(it) for _ in range(6))
    fn_ref = next(it) if final else None
    o_ref = next(it)
    h_scr, q_scr, oh_scr, y_scr, qg_scr, kg_scr, vg_scr = (next(it) for _ in range(7))

    c = pl.program_id(1)
    tt = x_ref.shape[0] // nbs
    seqs = k_ref.shape[0]
    d_head = wq_ref.shape[-1] // N_HEADS
    chunks = d_head // LANES
    n_chunks = N_HEADS * chunks
    n_mem = k_ref.shape[1] // n_chunks

    @pl.when(c == 0)
    def _():
        _norm_rows(h_scr, x_ref, nx_ref, BF16)
        _project_q(h_scr, wq_ref, q_scr, d_head)

    for bl in range(seqs):
        rows = pl.ds(c * seqs + bl, tt, stride=nbs)
        for j in range(n_chunks):
            pair, cs = _pair_slot(bl, j, chunks)
            cache_rows = pl.ds((j % chunks) * N_HEADS + j // chunks, n_mem, stride=n_chunks)
            qg_scr[pair, :, cs] = q_scr[j, rows, :]
            kg_scr[pair, :, cs] = k_ref[bl, cache_rows, :].astype(BF16)
            vg_scr[pair, :, cs] = v_ref[bl, cache_rows, :].astype(BF16)
    s = jnp.einsum("pqd,pkd->pqk", qg_scr[...].astype(BF16), kg_scr[...], preferred_element_type=F32)
    o = jnp.einsum("pqk,pkd->pqd", _softmax(s).astype(BF16), vg_scr[...], preferred_element_type=F32)
    for bl in range(seqs):
        rows = pl.ds(c * seqs + bl, tt, stride=nbs)
        for j in range(n_chunks):
            pair, cs = _pair_slot(bl, j, chunks)
            oh_scr[j, rows, :] = o[pair, :, cs]

    @pl.when(c == pl.num_programs(1) - 1)
    def _():
        o_all = jnp.concatenate([oh_scr[j] for j in range(n_chunks)], axis=1)
        out = x_ref[...] + _mm(o_all.astype(BF16), wo_ref[...])
        if final:
            y_scr[...] = out
            _norm_rows(o_ref, y_scr, fn_ref, F32)
        else:
            o_ref[...] = out


def _memory_kv_kernel(mem_ref, nmem_ref, wk_ref, wv_ref, kr_ref, vr_ref, kt_ref, vb_ref, h_scr):
    n_mem = mem_ref.shape[0]
    d_head = vb_ref.shape[-1]
    chunks = d_head // LANES
    n_chunks = chunks * N_HEADS
    _norm_rows(h_scr, mem_ref, nmem_ref, BF16)
    for w_ref, r_ref in ((wk_ref, kr_ref), (wv_ref, vr_ref)):
        for hd in range(N_HEADS):
            kh = _mm(h_scr[...], w_ref[:, hd * d_head:(hd + 1) * d_head])
            if r_ref is kr_ref:
                kt_ref[hd] = kh.T.astype(BF16)
            else:
                vb_ref[hd] = kh.astype(BF16)
            for cc in range(chunks):
                r_ref[pl.ds(cc * N_HEADS + hd, n_mem, stride=n_chunks), :] = kh[:, cc * LANES:(cc + 1) * LANES]


def _const_spec(shape, layer=None):
    if layer is None:
        return pl.BlockSpec(shape, lambda *_: (0,) * len(shape), pipeline_mode=pl.Buffered(1))
    return pl.BlockSpec((None,) + shape, lambda *_: (layer,) + (0,) * len(shape), pipeline_mode=pl.Buffered(1))


def _row_spec(rows, cols):
    return pl.BlockSpec((rows, cols), lambda i: (i, 0))


def _params(n_axes):
    return pltpu.CompilerParams(dimension_semantics=("arbitrary",) * n_axes, vmem_limit_bytes=VMEM_LIMIT_BYTES)


def _chunk_major(tm, d):
    return pltpu.VMEM((d // LANES, tm, LANES), F32)


def _even_layer(x, hp, hc, p, e, l, *, nb, tt, start_pos, carry):
    batch_major_in = x.ndim == 3
    d = x.shape[-1]
    tm = nb * tt
    rows = x.shape[0] * x.shape[1] if batch_major_in else x.shape[0]
    w_a, w_b = p["pool_scale"].shape[-1], p["conv_b"].shape[-1]
    hp_rows, hc_rows = POOL_HIST * nb, CONV_HIST * nb
    assert not carry or tm >= hc_rows
    assert not batch_major_in or (carry and x.shape[0] == nb)
    n_groups = len(POOL_WINDOWS)

    args = [x, hp, hc, p["norm_mix"], p["w_in_even"], p["pool_w"], p["pool_scale"], p["conv_w"], p["conv_b"],
            p["ln_g"], p["ln_b"], p["w_out_even"]]
    hist_spec = (lambda r, c: _const_spec((r, c))) if carry else _row_spec
    x_spec = pl.BlockSpec((nb, tt, d), lambda i: (0, i, 0)) if batch_major_in else _row_spec(tm, d)
    specs = [x_spec, hist_spec(hp_rows, w_a), hist_spec(hc_rows, w_b),
             _const_spec((1, d), l), _const_spec((d, 2 * w_a + 3 * w_b), e),
             _const_spec((n_groups, w_a // n_groups, w_a // n_groups), e), _const_spec((1, w_a), e),
             _const_spec((CONV_TAPS, w_b), e), _const_spec((1, w_b), e), _const_spec((1, w_b), e),
             _const_spec((1, w_b), e), _const_spec((w_a + w_b, d), e)]

    state_spec = (lambda r, c: pl.BlockSpec((r, c), lambda i: (0, 0))) if carry else _row_spec
    n_state = 1 if carry else rows // tm
    scratch = [pltpu.VMEM((tm, d), BF16), pltpu.VMEM((hp_rows + tm, w_a), F32), pltpu.VMEM((tm, w_b), F32),
               pltpu.VMEM((tm, w_a), F32), pltpu.VMEM((tm, w_b), F32), pltpu.VMEM((tm, w_b), F32),
               pltpu.VMEM((tm, w_a + w_b), BF16), pltpu.VMEM((CONV_TAPS * SUBLANES, w_b), F32)]
    if carry:
        scratch.append(pltpu.VMEM((hc_rows, w_b), F32))
    if batch_major_in:
        scratch += [_chunk_major(tm, d), pltpu.VMEM((tm, d), F32)]
    return pl.pallas_call(
        functools.partial(_even_kernel, nb=nb, tt=tt, start_pos=start_pos, carry=carry,
                          batch_major_in=batch_major_in),
        grid=(rows // tm,),
        in_specs=specs,
        out_specs=[_row_spec(tm, d), state_spec(hp_rows, w_a), state_spec(hc_rows, w_b)],
        out_shape=[jax.ShapeDtypeStruct((rows, d), F32), jax.ShapeDtypeStruct((n_state * hp_rows, w_a), F32),
                   jax.ShapeDtypeStruct((n_state * hc_rows, w_b), F32)],
        scratch_shapes=scratch,
        compiler_params=_params(1),
        name=f"even_mixer_{l}_{'carry' if carry else 'group'}",
    )(*args)


def _odd_layer(x, hs, p, o, l, *, nb, tt, carry):
    rows, d = x.shape
    tm = nb * tt
    w_c = p["short_w"].shape[-1]
    hs_rows = SHORT_HIST * nb
    assert not carry or tm >= hs_rows

    args = [x, hs, p["norm_mix"], p["w_in_odd"], p["short_w"], p["w_out_odd"]]
    hist_spec = (lambda r, c: _const_spec((r, c))) if carry else _row_spec
    specs = [_row_spec(tm, d), hist_spec(hs_rows, w_c), _const_spec((1, d), l), _const_spec((d, 4 * w_c), o),
             _const_spec((SHORT_TAPS, w_c), o), _const_spec((w_c, d), o)]
    state_spec = (lambda r, c: pl.BlockSpec((r, c), lambda i: (0, 0))) if carry else _row_spec
    n_state = 1 if carry else rows // tm
    scratch = [pltpu.VMEM((tm, d), BF16), pltpu.VMEM((tm, w_c), F32), pltpu.VMEM((tm, w_c), F32),
               pltpu.VMEM((tm, w_c), BF16)]
    if carry:
        scratch.append(pltpu.VMEM((hs_rows, w_c), F32))
    return pl.pallas_call(
        functools.partial(_odd_kernel, nb=nb, tt=tt, carry=carry),
        grid=(rows // tm,),
        in_specs=specs,
        out_specs=[_row_spec(tm, d), state_spec(hs_rows, w_c)],
        out_shape=[jax.ShapeDtypeStruct((rows, d), F32), jax.ShapeDtypeStruct((n_state * hs_rows, w_c), F32)],
        scratch_shapes=scratch,
        compiler_params=_params(1),
        name=f"odd_mixer_{l}_{'carry' if carry else 'group'}",
    )(*args)


def _xattn_prompt(x, kt, vb, p, l, *, nb, tt, final_norm=None, batch_major_out=False):
    rows, d = x.shape
    tm = nb * tt
    n_pairs, d_head = nb * N_HEADS, d // N_HEADS
    assert kt.shape[1] == n_pairs and vb.shape[1] == n_pairs
    final = final_norm is not None
    args = [x, p["norm_x"], p["w_q"], kt, vb, p["w_o"]]
    specs = [_row_spec(tm, d), _const_spec((1, d), l), _const_spec((d, d), l), _const_spec(kt.shape[1:], l),
             _const_spec(vb.shape[1:], l), _const_spec((d, d), l)]
    if final:
        args, specs = args + [final_norm], specs + [_const_spec((1, d))]
    if batch_major_out:
        out_spec = pl.BlockSpec((nb, tt, d), lambda i: (0, i, 0))
        out_shape = jax.ShapeDtypeStruct((nb, rows // nb, d), F32)
    else:
        out_spec, out_shape = _row_spec(tm, d), jax.ShapeDtypeStruct((rows, d), F32)
    return pl.pallas_call(
        functools.partial(_xattn_prompt_kernel, nb=nb, tt=tt, final=final, batch_major_out=batch_major_out),
        grid=(rows // tm,),
        in_specs=specs,
        out_specs=out_spec,
        out_shape=out_shape,
        scratch_shapes=[pltpu.VMEM((tm, d), BF16), _chunk_major(tm, d), _chunk_major(tm, d),
                        pltpu.VMEM((n_pairs, tt, d_head), BF16), pltpu.VMEM((tm, d), F32)],
        compiler_params=_params(1),
        name=f"xattn_prompt_{l}",
    )(*args)


def _xattn_sample(x, kr, vr, p, l, *, nbs, tm, final_norm=None):
    rows, d = x.shape
    n_chunks = nbs // XATTN_SEQS
    final = final_norm is not None
    d_head = d // N_HEADS
    n_pairs = XATTN_SEQS * N_HEADS
    n_mem = kr.shape[2] * LANES // d
    cache_block = (XATTN_SEQS,) + kr.shape[2:]
    cache_spec = pl.BlockSpec((None,) + cache_block, lambda g, c: (l, g * n_chunks + c, 0, 0))
    args = [x, p["norm_x"], p["w_q"], kr, vr, p["w_o"]]
    specs = [pl.BlockSpec((tm, d), lambda g, c: (g, 0)), _const_spec((1, d), l), _const_spec((d, d), l),
             cache_spec, cache_spec, _const_spec((d, d), l)]
    if final:
        args, specs = args + [final_norm], specs + [_const_spec((1, d))]
    return pl.pallas_call(
        functools.partial(_xattn_sample_kernel, nbs=nbs, final=final),
        grid=(rows // tm, n_chunks),
        in_specs=specs,
        out_specs=pl.BlockSpec((tm, d), lambda g, c: (g, 0)),
        out_shape=jax.ShapeDtypeStruct((rows, d), F32),
        scratch_shapes=[pltpu.VMEM((tm, d), BF16), _chunk_major(tm, d), _chunk_major(tm, d), pltpu.VMEM((tm, d), F32),
                        pltpu.VMEM((n_pairs, tm // nbs, d_head), F32), pltpu.VMEM((n_pairs, n_mem, d_head), BF16),
                        pltpu.VMEM((n_pairs, n_mem, d_head), BF16)],
        compiler_params=_params(2),
        name=f"xattn_sample_{l}",
    )(*args)


def _memory_kv(mem, p, n_seq, n_mem, d_head):
    d = mem.shape[-1]
    depth = p["w_k"].shape[0]
    r_rows = n_mem * d // LANES
    r_spec = pl.BlockSpec((None, None, r_rows, LANES), lambda l, b: (l, b, 0, 0))
    kt_spec = pl.BlockSpec((None, None, N_HEADS, d_head, n_mem), lambda l, b: (l, b, 0, 0, 0))
    vb_spec = pl.BlockSpec((None, None, N_HEADS, n_mem, d_head), lambda l, b: (l, b, 0, 0, 0))
    w_spec = pl.BlockSpec((None, d, d), lambda l, b: (l, 0, 0))
    r_shape = jax.ShapeDtypeStruct((depth, n_seq, r_rows, LANES), F32)
    return pl.pallas_call(
        _memory_kv_kernel,
        grid=(depth, n_seq),
        in_specs=[pl.BlockSpec((n_mem, d), lambda l, b: (b, 0)), pl.BlockSpec((None, 1, d), lambda l, b: (l, 0, 0)),
                  w_spec, w_spec],
        out_specs=[r_spec, r_spec, kt_spec, vb_spec],
        out_shape=[r_shape, r_shape, jax.ShapeDtypeStruct((depth, n_seq, N_HEADS, d_head, n_mem), BF16),
                   jax.ShapeDtypeStruct((depth, n_seq, N_HEADS, n_mem, d_head), BF16)],
        scratch_shapes=[pltpu.VMEM((n_mem, d), BF16)],
        compiler_params=_params(2),
        name="memory_kv",
    )(mem, p["norm_mem"], p["w_k"], p["w_v"])


def _to_time_major(x, nb):
    b, t, c = x.shape
    return x.reshape(b // nb, nb, t, c).transpose(0, 2, 1, 3).reshape(b * t, c)


def _from_time_major(y, b, t, nb):
    c = y.shape[-1]
    return y.reshape(b // nb, t, nb, c).transpose(0, 2, 1, 3).reshape(b, t, c)


def _cache_rows(c):
    *lead, n_mem, heads, d_head = c.shape
    chunks = d_head // LANES
    r = c.reshape(*lead, n_mem, heads, chunks, LANES)
    return jnp.swapaxes(r, -3, -2).reshape(*lead, n_mem * chunks * heads, LANES)


def _cache_from_rows(r, n_mem, heads, d_head):
    *lead, _, _ = r.shape
    chunks = d_head // LANES
    c = r.reshape(*lead, n_mem, chunks, heads, LANES)
    return jnp.swapaxes(c, -3, -2).reshape(*lead, n_mem, heads, d_head)


def kernel(x_prompt, x_sample, state_pool, state_conv, state_short, cache_mem_k, cache_mem_v, mem_prompt, norm_mix, w_in_even, pool_w, pool_scale, conv_w, conv_b, ln_g, ln_b, w_out_even, w_in_odd, short_w, w_out_odd, norm_x, norm_mem, w_q, w_k, w_v, w_o, final_norm):
    b_p, t_p, d = x_prompt.shape
    b_s, t_s, _ = x_sample.shape
    depth = norm_mix.shape[0]
    n_mem = mem_prompt.shape[1]
    heads, d_head = cache_mem_k.shape[-2:]
    assert heads == N_HEADS and d_head % LANES == 0 and depth >= 2
    assert d_head & (d_head - 1) == 0 and d_head.bit_length() % 2 == 1, "the score scale must be a power of two"

    nb_p, tt_p, tt_a = b_p, TILE_ROWS // b_p, ATTN_TILE_ROWS // b_p
    nb_s, tt_s = TILE_ROWS // t_s, t_s
    for nb in (nb_p, nb_s):
        assert nb % SUBLANES == 0 and nb & (nb - 1) == 0
    assert t_p % tt_p == 0 and t_p % tt_a == 0 and b_s % nb_s == 0 and nb_s % XATTN_SEQS == 0

    row = lambda a: a.reshape(a.shape[0], 1, a.shape[-1])
    p = {
        "norm_mix": row(norm_mix), "norm_x": row(norm_x), "norm_mem": row(norm_mem),
        "w_in_even": w_in_even.astype(BF16), "pool_w": pool_w.astype(BF16), "pool_scale": row(pool_scale),
        "conv_w": conv_w, "conv_b": row(conv_b), "ln_g": row(ln_g), "ln_b": row(ln_b),
        "w_out_even": w_out_even.astype(BF16),
        "w_in_odd": w_in_odd.astype(BF16), "short_w": short_w, "w_out_odd": w_out_odd.astype(BF16),
        "w_q": w_q.astype(BF16), "w_k": w_k.astype(BF16), "w_v": w_v.astype(BF16), "w_o": w_o.astype(BF16),
    }
    fnorm = final_norm.reshape(1, d)
    w_a, w_b, w_c = pool_scale.shape[-1], conv_b.shape[-1], short_w.shape[-1]

    kr, vr, kt, vb = _memory_kv(mem_prompt.reshape(b_p * n_mem, d), p, b_p, n_mem, d_head)
    mem_k_p = _cache_from_rows(kr, n_mem, heads, d_head)
    mem_v_p = _cache_from_rows(vr, n_mem, heads, d_head)
    kt = kt.reshape(depth, b_p * heads, d_head, n_mem)
    vb = vb.reshape(depth, b_p * heads, n_mem, d_head)

    xp = x_prompt
    zp = jnp.zeros((POOL_HIST * nb_p, w_a), F32)
    zc = jnp.zeros((CONV_HIST * nb_p, w_b), F32)
    zs = jnp.zeros((SHORT_HIST * nb_p, w_c), F32)
    pool_p, conv_p, short_p = [], [], []
    for l in range(depth):
        last = l == depth - 1
        if l % 2 == 0:
            xp, sp, sc = _even_layer(xp, zp, zc, p, l // 2, l, nb=nb_p, tt=tt_p, start_pos=0, carry=True)
            pool_p.append(_from_time_major(sp, b_p, POOL_HIST, nb_p))
            conv_p.append(_from_time_major(sc, b_p, CONV_HIST, nb_p))
        else:
            xp, ss = _odd_layer(xp, zs, p, l // 2, l, nb=nb_p, tt=tt_p, carry=True)
            short_p.append(_from_time_major(ss, b_p, SHORT_HIST, nb_p))
        xp = _xattn_prompt(xp, kt, vb, p, l, nb=nb_p, tt=tt_a, final_norm=fnorm if last else None,
                           batch_major_out=last)
    y_prompt = xp

    krs, vrs = _cache_rows(cache_mem_k), _cache_rows(cache_mem_v)
    xs = _to_time_major(x_sample, nb_s)
    pool_s, conv_s, short_s = [], [], []
    for l in range(depth):
        if l % 2 == 0:
            e = l // 2
            xs, sp, sc = _even_layer(xs, _to_time_major(state_pool[e], nb_s), _to_time_major(state_conv[e], nb_s),
                                     p, e, l, nb=nb_s, tt=tt_s, start_pos=PAST_LEN, carry=False)
            pool_s.append(_from_time_major(sp, b_s, POOL_HIST, nb_s))
            conv_s.append(_from_time_major(sc, b_s, CONV_HIST, nb_s))
        else:
            o = l // 2
            xs, ss = _odd_layer(xs, _to_time_major(state_short[o], nb_s), p, o, l, nb=nb_s, tt=tt_s, carry=False)
            short_s.append(_from_time_major(ss, b_s, SHORT_HIST, nb_s))
        xs = _xattn_sample(xs, krs, vrs, p, l, nbs=nb_s, tm=nb_s * tt_s,
                           final_norm=fnorm if l == depth - 1 else None)
    y_sample = _from_time_major(xs, b_s, t_s, nb_s)

    return (y_prompt, y_sample, jnp.stack(pool_p), jnp.stack(conv_p), jnp.stack(short_p), mem_k_p, mem_v_p,
            jnp.stack(pool_s), jnp.stack(conv_s), jnp.stack(short_s))
```

```python
import functools

import jax
import jax.numpy as jnp
from jax import lax
from jax.experimental import pallas as pl
from jax.experimental.pallas import tpu as pltpu

F32 = jnp.float32
BF16 = jnp.bfloat16

EPS = 1e-6
POOL_WINDOWS = (2, 4, 8, 16)
POOL_HIST = max(POOL_WINDOWS) - 1
CONV_TAPS = 31
CONV_HIST = CONV_TAPS - 1
SHORT_TAPS = 3
SHORT_HIST = SHORT_TAPS - 1
N_HEADS = 4
PAST_LEN = 16384

LANES = 128
SUBLANES = 8
MXU_COLS = 256
VMEM_LIMIT_BYTES = 56 * 1024 * 1024

TILE_ROWS = 512
SAMPLE_TILE_ROWS = 256
ATTN_TILE_ROWS = 1024
NORM_ROWS = 32
CONV_ROWS = 128
CONV_TAP_GROUP = 8
XATTN_SEQS = 8


def _mm(a, b):
    return jnp.dot(a, b, preferred_element_type=F32)


def _silu(x):
    return x * jax.nn.sigmoid(x)


def _rmsnorm(x, g):
    ms = jnp.mean(x * x, axis=-1, keepdims=True)
    return x * lax.rsqrt(ms + EPS) * g


def _softmax(s):
    e = jnp.exp(s - jnp.max(s, axis=-1, keepdims=True))
    return e / jnp.sum(e, axis=-1, keepdims=True)


def _norm_rows(dst_ref, src_ref, g_ref, dtype):
    rows = src_ref.shape[0]
    for r0 in range(0, rows, NORM_ROWS):
        rs = slice(r0, r0 + NORM_ROWS)
        dst_ref[rs, :] = _rmsnorm(src_ref[rs, :], g_ref[...]).astype(dtype)


def _window_rows(hist_ref, new_ref, start, n_rows, cols):
    hist_rows = hist_ref.shape[0]
    n_hist = min(max(hist_rows - start, 0), n_rows)
    parts = []
    if n_hist > 0:
        parts.append(hist_ref[start:start + n_hist, cols])
    if n_hist < n_rows:
        s = start + n_hist - hist_rows
        parts.append(new_ref[s:s + n_rows - n_hist, cols])
    return parts[0] if len(parts) == 1 else jnp.concatenate(parts, axis=0)


def _last_rows(dst_ref, hist_ref, new_ref):
    n, tm = dst_ref.shape[0], new_ref.shape[0]
    if tm >= n:
        dst_ref[...] = new_ref[tm - n:tm, :]
    else:
        dst_ref[0:n - tm, :] = hist_ref[tm:n, :]
        dst_ref[n - tm:n, :] = new_ref[...]


def _exact_zero_of(x):
    bits = lax.bitcast_convert_type(x, jnp.uint32)
    sixteen = jnp.full(bits.shape, 16, jnp.uint32)
    bits = lax.shift_right_logical(lax.shift_right_logical(bits, sixteen), sixteen)
    return lax.bitcast_convert_type(bits, F32)


def _lane_chunks(width):
    return [slice(c0, c0 + LANES) for c0 in range(0, width, LANES)]


def _even_kernel(*refs, nb, tt, start_pos, carry, batch_major_in):
    tm = nb * tt
    it = iter(refs)
    x_ref, hp_ref, hc_ref, nmix_ref, win_ref, poolw_ref, pscale_ref = (next(it) for _ in range(7))
    convw_ref, convb_ref, lng_ref, lnb_ref, wout_ref = (next(it) for _ in range(5))
    o_ref, newp_ref, newc_ref = next(it), next(it), next(it)
    h_scr, pool_scr, vnew_scr, ag_scr, bg_scr, c_scr, ac_scr, wtile_scr = (next(it) for _ in range(8))
    hc_scr = next(it) if carry else None
    if batch_major_in:
        xc_scr, xt_scr = next(it), next(it)

    i = pl.program_id(0)
    n_steps = pl.num_programs(0)
    w_a = pscale_ref.shape[-1]
    w_b = convb_ref.shape[-1]
    hp_rows = POOL_HIST * nb
    pool_group = w_a // len(POOL_WINDOWS)

    if carry:
        @pl.when(i == 0)
        def _():
            pool_scr[0:hp_rows, :] = hp_ref[...]
            hc_scr[...] = hc_ref[...]
        hist_c = hc_scr
    else:
        pool_scr[0:hp_rows, :] = hp_ref[...]
        hist_c = hc_ref

    if batch_major_in:
        for b in range(nb):
            for j, cs in enumerate(_lane_chunks(x_ref.shape[-1])):
                xc_scr[j, pl.ds(b, tt, stride=nb), :] = x_ref[b, :, cs]
        for j, cs in enumerate(_lane_chunks(x_ref.shape[-1])):
            xt_scr[:, cs] = xc_scr[j]
        x_src = xt_scr
    else:
        x_src = x_ref

    _norm_rows(h_scr, x_src, nmix_ref, BF16)

    bval = _mm(h_scr[...], win_ref[:, 2 * w_a:2 * w_a + w_b])
    bglu = _mm(h_scr[...], win_ref[:, 2 * w_a + w_b:2 * w_a + 2 * w_b])
    vnew_scr[...] = bval * jax.nn.sigmoid(bglu)

    def project(dst_ref, dst_rows, col0, c0):
        def run():
            dst_ref[dst_rows, c0:c0 + MXU_COLS] = _mm(h_scr[...], win_ref[:, col0 + c0:col0 + c0 + MXU_COLS])
        return run
    pieces = ([project(pool_scr, slice(hp_rows, hp_rows + tm), 0, c0) for c0 in range(0, w_a, MXU_COLS)]
              + [project(ag_scr, slice(None), w_a, c0) for c0 in range(0, w_a, MXU_COLS)]
              + [project(bg_scr, slice(None), 2 * w_a + 2 * w_b, c0) for c0 in range(0, w_b, MXU_COLS)])
    conv_chunks = _lane_chunks(w_b)
    for k in range(CONV_TAPS):
        wtile_scr[k * SUBLANES:(k + 1) * SUBLANES, :] = jnp.broadcast_to(convw_ref[k:k + 1, :], (SUBLANES, w_b))

    def conv_chunk(n, cs):
        order = None
        for r0 in range(0, tm, CONV_ROWS):
            bias = jnp.broadcast_to(convb_ref[:, cs], (SUBLANES, LANES))
            if order is not None:
                bias = bias + order
            acc = jnp.concatenate([bias] * (CONV_ROWS // SUBLANES), axis=0)
            for k0 in range(0, CONV_TAPS, CONV_TAP_GROUP):
                taps = range(k0, min(k0 + CONV_TAP_GROUP, CONV_TAPS))
                win = _window_rows(hist_c, vnew_scr, r0 + k0 * nb, (len(taps) - 1) * nb + CONV_ROWS, cs)
                for k in taps:
                    wk = wtile_scr[k * SUBLANES:(k + 1) * SUBLANES, cs]
                    acc = acc + (win[(k - k0) * nb:(k - k0) * nb + CONV_ROWS, :]
                                 * jnp.concatenate([wk] * (CONV_ROWS // SUBLANES), axis=0))
            c_scr[r0:r0 + CONV_ROWS, cs] = acc
            order = _exact_zero_of(acc[0:SUBLANES, :])
        for run in pieces[n * len(pieces) // len(conv_chunks):(n + 1) * len(pieces) // len(conv_chunks)]:
            run()

    for n, cs in enumerate(conv_chunks):
        pl.when(i >= 0)(functools.partial(conv_chunk, n, cs))

    row = lax.broadcasted_iota(jnp.int32, (tm, LANES), 0)
    step = row >> (nb.bit_length() - 1)
    pos1 = step + (start_pos + 1 + (i * tt if carry else 0))
    for g, w in enumerate(POOL_WINDOWS):
        cnt = jnp.minimum(pos1, w).astype(F32)
        a_parts = []
        for c0 in range(g * pool_group, (g + 1) * pool_group, LANES):
            e = pool_scr[(POOL_HIST + 1 - w) * nb:hp_rows + tm, c0:c0 + LANES]
            s, sh = e, 1
            while sh < w:
                n = s.shape[0]
                s = s[sh * nb:, :] + s[:n - sh * nb, :]
                sh *= 2
            a_parts.append((s / cnt - e[(w - 1) * nb:, :]).astype(BF16))
        cs = slice(g * pool_group, (g + 1) * pool_group)
        pa = _mm(jnp.concatenate(a_parts, axis=1), poolw_ref[g])
        ac_scr[:, cs] = (pa * pscale_ref[:, cs] * _silu(ag_scr[:, cs])).astype(BF16)

    for r0 in range(0, tm, NORM_ROWS):
        rs = slice(r0, r0 + NORM_ROWS)
        c = c_scr[rs, :]
        d = c - jnp.mean(c, axis=-1, keepdims=True)
        var = jnp.mean(d * d, axis=-1, keepdims=True)
        y = d * lax.rsqrt(var + EPS) * lng_ref[...] + lnb_ref[...]
        ac_scr[rs, w_a:w_a + w_b] = (_silu(y) * _silu(bg_scr[rs, :])).astype(BF16)

    o_ref[...] = x_src[...] + _mm(ac_scr[...], wout_ref[...])

    if carry:
        _last_rows(hc_scr, hc_scr, vnew_scr)
        hp_new = pool_scr[tm:tm + hp_rows, :]
        pool_scr[0:hp_rows, :] = hp_new

        @pl.when(i == n_steps - 1)
        def _():
            newp_ref[...] = pool_scr[0:hp_rows, :]
            newc_ref[...] = hc_scr[...]
    else:
        newp_ref[...] = pool_scr[tm:tm + hp_rows, :]
        _last_rows(newc_ref, hc_ref, vnew_scr)


def _odd_kernel(*refs, nb, tt, carry):
    tm = nb * tt
    it = iter(refs)
    x_ref, hs_ref, nmix_ref, win_ref, shortw_ref, wout_ref = (next(it) for _ in range(6))
    o_ref, news_ref = next(it), next(it)
    h_scr, unew_scr, gate_scr, y_scr = (next(it) for _ in range(4))
    hs_scr = next(it) if carry else None

    i = pl.program_id(0)
    n_steps = pl.num_programs(0)
    w_c = shortw_ref.shape[-1]
    d_model = x_ref.shape[-1]

    if carry:
        @pl.when(i == 0)
        def _():
            hs_scr[...] = hs_ref[...]
        hist = hs_scr
    else:
        hist = hs_ref

    _norm_rows(h_scr, x_ref, nmix_ref, BF16)

    for c0 in range(0, w_c, d_model):
        cs = slice(c0, c0 + d_model)
        gc = _mm(h_scr[...], win_ref[:, w_c + c0:w_c + c0 + d_model])
        xv = _mm(h_scr[...], win_ref[:, 2 * w_c + c0:2 * w_c + c0 + d_model])
        unew_scr[:, cs] = gc * xv
    for c0 in range(0, w_c, d_model):
        cs = slice(c0, c0 + d_model)
        gb = _mm(h_scr[...], win_ref[:, c0:c0 + d_model])
        gg = _mm(h_scr[...], win_ref[:, 3 * w_c + c0:3 * w_c + c0 + d_model])
        gate_scr[:, cs] = gb * _silu(gg)
    for cs in _lane_chunks(w_c):
        acc = _window_rows(hist, unew_scr, 0, tm, cs) * shortw_ref[0:1, cs]
        for k in range(1, SHORT_TAPS):
            acc = acc + _window_rows(hist, unew_scr, k * nb, tm, cs) * shortw_ref[k:k + 1, cs]
        y_scr[:, cs] = (gate_scr[:, cs] * acc).astype(BF16)

    o_ref[...] = x_ref[...] + _mm(y_scr[...], wout_ref[...])

    if carry:
        _last_rows(hs_scr, hs_scr, unew_scr)

        @pl.when(i == n_steps - 1)
        def _():
            news_ref[...] = hs_scr[...]
    else:
        _last_rows(news_ref, hs_ref, unew_scr)


def _project_q(h_scr, wq_ref, q_scr, d_head):
    chunks = d_head // LANES
    for hd in range(N_HEADS):
        q = _mm(h_scr[...], wq_ref[:, hd * d_head:(hd + 1) * d_head]) * d_head ** -0.5
        for c in range(chunks):
            q_scr[hd * chunks + c] = q[:, c * LANES:(c + 1) * LANES]


def _pair_slot(seq, j, chunks):
    return seq * N_HEADS + j // chunks, slice((j % chunks) * LANES, (j % chunks + 1) * LANES)


def _xattn_prompt_kernel(*refs, nb, tt, final, batch_major_out):
    it = iter(refs)
    x_ref, nx_ref, wq_ref, kt_ref, v_ref, wo_ref = (next(it) for _ in range(6))
    fn_ref = next(it) if final else None
    o_ref = next(it)
    h_scr, q_scr, oh_scr, qg_scr, y_scr = (next(it) for _ in range(5))
    d_head = v_ref.shape[-1]
    chunks = d_head // LANES
    n_chunks = N_HEADS * chunks

    _norm_rows(h_scr, x_ref, nx_ref, BF16)
    _project_q(h_scr, wq_ref, q_scr, d_head)
    for b in range(nb):
        for j in range(n_chunks):
            pair, cs = _pair_slot(b, j, chunks)
            qg_scr[pair, :, cs] = q_scr[j, pl.ds(b, tt, stride=nb), :].astype(BF16)
    s = jnp.einsum("pqd,pdk->pqk", qg_scr[...], kt_ref[...], preferred_element_type=F32)
    o = jnp.einsum("pqk,pkd->pqd", _softmax(s).astype(BF16), v_ref[...], preferred_element_type=F32)
    for b in range(nb):
        for j in range(n_chunks):
            pair, cs = _pair_slot(b, j, chunks)
            oh_scr[j, pl.ds(b, tt, stride=nb), :] = o[pair, :, cs]
    o_all = jnp.concatenate([oh_scr[j] for j in range(n_chunks)], axis=1)
    out = x_ref[...] + _mm(o_all.astype(BF16), wo_ref[...])

    if not (final or batch_major_out):
        o_ref[...] = out
        return
    y_scr[...] = out
    if final:
        _norm_rows(y_scr, y_scr, fn_ref, F32)
    if not batch_major_out:
        o_ref[...] = y_scr[...]
        return
    for j, cs in enumerate(_lane_chunks(y_scr.shape[-1])):
        q_scr[j] = y_scr[:, cs]
    for b in range(nb):
        for j, cs in enumerate(_lane_chunks(y_scr.shape[-1])):
            o_ref[b, :, cs] = q_scr[j, pl.ds(b, tt, stride=nb), :]


def _xattn_sample_kernel(*refs, nbs, final):
    it = iter(refs)
    x_ref, nx_ref, wq_ref, k_ref, v_ref, wo_ref = (next(it) for _ in range(6))
    fn_ref = next(it) if final else None
    o_ref = next(it)
    h_scr, q_scr, oh_scr, y_scr, qg_scr, kg_scr, vg_scr = (next(it) for _ in range(7))

    c = pl.program_id(1)
    tt = x_ref.shape[0] // nbs
    seqs = k_ref.shape[0]
    d_head = wq_ref.shape[-1] // N_HEADS
    chunks = d_head // LANES
    n_chunks = N_HEADS * chunks
    n_mem = k_ref.shape[1] // n_chunks

    @pl.when(c == 0)
    def _():
        _norm_rows(h_scr, x_ref, nx_ref, BF16)
        _project_q(h_scr, wq_ref, q_scr, d_head)

    for bl in range(seqs):
        rows = pl.ds(c * seqs + bl, tt, stride=nbs)
        for j in range(n_chunks):
            pair, cs = _pair_slot(bl, j, chunks)
            cache_rows = pl.ds((j % chunks) * N_HEADS + j // chunks, n_mem, stride=n_chunks)
            qg_scr[pair, :, cs] = q_scr[j, rows, :]
            kg_scr[pair, :, cs] = k_ref[bl, cache_rows, :].astype(BF16)
            vg_scr[pair, :, cs] = v_ref[bl, cache_rows, :].astype(BF16)
    s = jnp.einsum("pqd,pkd->pqk", qg_scr[...].astype(BF16), kg_scr[...], preferred_element_type=F32)
    o = jnp.einsum("pqk,pkd->pqd", _softmax(s).astype(BF16), vg_scr[...], preferred_element_type=F32)
    for bl in range(seqs):
        rows = pl.ds(c * seqs + bl, tt, stride=nbs)
        for j in range(n_chunks):
            pair, cs = _pair_slot(bl, j, chunks)
            oh_scr[j, rows, :] = o[pair, :, cs]

    @pl.when(c == pl.num_programs(1) - 1)
    def _():
        o_all = jnp.concatenate([oh_scr[j] for j in range(n_chunks)], axis=1)
        out = x_ref[...] + _mm(o_all.astype(BF16), wo_ref[...])
        if final:
            y_scr[...] = out
            _norm_rows(o_ref, y_scr, fn_ref, F32)
        else:
            o_ref[...] = out


def _memory_kv_kernel(mem_ref, nmem_ref, wk_ref, wv_ref, kr_ref, vr_ref, kt_ref, vb_ref, h_scr):
    n_mem = mem_ref.shape[0]
    d_head = vb_ref.shape[-1]
    chunks = d_head // LANES
    n_chunks = chunks * N_HEADS
    _norm_rows(h_scr, mem_ref, nmem_ref, BF16)
    for w_ref, r_ref in ((wk_ref, kr_ref), (wv_ref, vr_ref)):
        for hd in range(N_HEADS):
            kh = _mm(h_scr[...], w_ref[:, hd * d_head:(hd + 1) * d_head])
            if r_ref is kr_ref:
                kt_ref[hd] = kh.T.astype(BF16)
            else:
                vb_ref[hd] = kh.astype(BF16)
            for cc in range(chunks):
                r_ref[pl.ds(cc * N_HEADS + hd, n_mem, stride=n_chunks), :] = kh[:, cc * LANES:(cc + 1) * LANES]


def _const_spec(shape, layer=None):
    if layer is None:
        return pl.BlockSpec(shape, lambda *_: (0,) * len(shape), pipeline_mode=pl.Buffered(1))
    return pl.BlockSpec((None,) + shape, lambda *_: (layer,) + (0,) * len(shape), pipeline_mode=pl.Buffered(1))


def _row_spec(rows, cols):
    return pl.BlockSpec((rows, cols), lambda i: (i, 0))


def _params(n_axes, input_fusion=None):
    return pltpu.CompilerParams(dimension_semantics=("arbitrary",) * n_axes, vmem_limit_bytes=VMEM_LIMIT_BYTES,
                                allow_input_fusion=input_fusion)


def _chunk_major(tm, d):
    return pltpu.VMEM((d // LANES, tm, LANES), F32)


def _even_layer(x, hp, hc, p, e, l, *, nb, tt, start_pos, carry):
    batch_major_in = x.ndim == 3
    d = x.shape[-1]
    tm = nb * tt
    rows = x.shape[0] * x.shape[1] if batch_major_in else x.shape[0]
    w_a, w_b = p["pool_scale"].shape[-1], p["conv_b"].shape[-1]
    hp_rows, hc_rows = POOL_HIST * nb, CONV_HIST * nb
    assert not carry or tm >= hc_rows
    assert not batch_major_in or (carry and x.shape[0] == nb)
    n_groups = len(POOL_WINDOWS)

    args = [x, hp, hc, p["norm_mix"], p["w_in_even"], p["pool_w"], p["pool_scale"], p["conv_w"], p["conv_b"],
            p["ln_g"], p["ln_b"], p["w_out_even"]]
    hist_spec = (lambda r, c: _const_spec((r, c))) if carry else _row_spec
    x_spec = pl.BlockSpec((nb, tt, d), lambda i: (0, i, 0)) if batch_major_in else _row_spec(tm, d)
    specs = [x_spec, hist_spec(hp_rows, w_a), hist_spec(hc_rows, w_b),
             _const_spec((1, d), l), _const_spec((d, 2 * w_a + 3 * w_b), e),
             _const_spec((n_groups, w_a // n_groups, w_a // n_groups), e), _const_spec((1, w_a), e),
             _const_spec((CONV_TAPS, w_b), e), _const_spec((1, w_b), e), _const_spec((1, w_b), e),
             _const_spec((1, w_b), e), _const_spec((w_a + w_b, d), e)]

    state_spec = (lambda r, c: pl.BlockSpec((r, c), lambda i: (0, 0))) if carry else _row_spec
    n_state = 1 if carry else rows // tm
    scratch = [pltpu.VMEM((tm, d), BF16), pltpu.VMEM((hp_rows + tm, w_a), F32), pltpu.VMEM((tm, w_b), F32),
               pltpu.VMEM((tm, w_a), F32), pltpu.VMEM((tm, w_b), F32), pltpu.VMEM((tm, w_b), F32),
               pltpu.VMEM((tm, w_a + w_b), BF16), pltpu.VMEM((CONV_TAPS * SUBLANES, w_b), F32)]
    if carry:
        scratch.append(pltpu.VMEM((hc_rows, w_b), F32))
    if batch_major_in:
        scratch += [_chunk_major(tm, d), pltpu.VMEM((tm, d), F32)]
    return pl.pallas_call(
        functools.partial(_even_kernel, nb=nb, tt=tt, start_pos=start_pos, carry=carry,
                          batch_major_in=batch_major_in),
        grid=(rows // tm,),
        in_specs=specs,
        out_specs=[_row_spec(tm, d), state_spec(hp_rows, w_a), state_spec(hc_rows, w_b)],
        out_shape=[jax.ShapeDtypeStruct((rows, d), F32), jax.ShapeDtypeStruct((n_state * hp_rows, w_a), F32),
                   jax.ShapeDtypeStruct((n_state * hc_rows, w_b), F32)],
        scratch_shapes=scratch,
        compiler_params=_params(1),
        name=f"even_mixer_{l}_{'carry' if carry else 'group'}",
    )(*args)


def _odd_layer(x, hs, p, o, l, *, nb, tt, carry):
    rows, d = x.shape
    tm = nb * tt
    w_c = p["short_w"].shape[-1]
    hs_rows = SHORT_HIST * nb
    assert not carry or tm >= hs_rows

    args = [x, hs, p["norm_mix"], p["w_in_odd"], p["short_w"], p["w_out_odd"]]
    hist_spec = (lambda r, c: _const_spec((r, c))) if carry else _row_spec
    specs = [_row_spec(tm, d), hist_spec(hs_rows, w_c), _const_spec((1, d), l), _const_spec((d, 4 * w_c), o),
             _const_spec((SHORT_TAPS, w_c), o), _const_spec((w_c, d), o)]
    state_spec = (lambda r, c: pl.BlockSpec((r, c), lambda i: (0, 0))) if carry else _row_spec
    n_state = 1 if carry else rows // tm
    scratch = [pltpu.VMEM((tm, d), BF16), pltpu.VMEM((tm, w_c), F32), pltpu.VMEM((tm, w_c), F32),
               pltpu.VMEM((tm, w_c), BF16)]
    if carry:
        scratch.append(pltpu.VMEM((hs_rows, w_c), F32))
    return pl.pallas_call(
        functools.partial(_odd_kernel, nb=nb, tt=tt, carry=carry),
        grid=(rows // tm,),
        in_specs=specs,
        out_specs=[_row_spec(tm, d), state_spec(hs_rows, w_c)],
        out_shape=[jax.ShapeDtypeStruct((rows, d), F32), jax.ShapeDtypeStruct((n_state * hs_rows, w_c), F32)],
        scratch_shapes=scratch,
        compiler_params=_params(1),
        name=f"odd_mixer_{l}_{'carry' if carry else 'group'}",
    )(*args)


def _xattn_prompt(x, kt, vb, p, l, *, nb, tt, final_norm=None, batch_major_out=False):
    rows, d = x.shape
    tm = nb * tt
    n_pairs, d_head = nb * N_HEADS, d // N_HEADS
    assert kt.shape[1] == n_pairs and vb.shape[1] == n_pairs
    final = final_norm is not None
    args = [x, p["norm_x"], p["w_q"], kt, vb, p["w_o"]]
    specs = [_row_spec(tm, d), _const_spec((1, d), l), _const_spec((d, d), l), _const_spec(kt.shape[1:], l),
             _const_spec(vb.shape[1:], l), _const_spec((d, d), l)]
    if final:
        args, specs = args + [final_norm], specs + [_const_spec((1, d))]
    if batch_major_out:
        out_spec = pl.BlockSpec((nb, tt, d), lambda i: (0, i, 0))
        out_shape = jax.ShapeDtypeStruct((nb, rows // nb, d), F32)
    else:
        out_spec, out_shape = _row_spec(tm, d), jax.ShapeDtypeStruct((rows, d), F32)
    return pl.pallas_call(
        functools.partial(_xattn_prompt_kernel, nb=nb, tt=tt, final=final, batch_major_out=batch_major_out),
        grid=(rows // tm,),
        in_specs=specs,
        out_specs=out_spec,
        out_shape=out_shape,
        scratch_shapes=[pltpu.VMEM((tm, d), BF16), _chunk_major(tm, d), _chunk_major(tm, d),
                        pltpu.VMEM((n_pairs, tt, d_head), BF16), pltpu.VMEM((tm, d), F32)],
        compiler_params=_params(1),
        name=f"xattn_prompt_{l}",
    )(*args)


def _xattn_sample(x, kr, vr, p, l, *, nbs, tm, final_norm=None):
    rows, d = x.shape
    n_chunks = nbs // XATTN_SEQS
    final = final_norm is not None
    d_head = d // N_HEADS
    n_pairs = XATTN_SEQS * N_HEADS
    n_mem = kr.shape[2] * LANES // d
    cache_block = (XATTN_SEQS,) + kr.shape[2:]
    cache_spec = pl.BlockSpec((None,) + cache_block, lambda g, c: (l, g * n_chunks + c, 0, 0))
    args = [x, p["norm_x"], p["w_q"], kr, vr, p["w_o"]]
    specs = [pl.BlockSpec((tm, d), lambda g, c: (g, 0)), _const_spec((1, d), l), _const_spec((d, d), l),
             cache_spec, cache_spec, _const_spec((d, d), l)]
    if final:
        args, specs = args + [final_norm], specs + [_const_spec((1, d))]
    return pl.pallas_call(
        functools.partial(_xattn_sample_kernel, nbs=nbs, final=final),
        grid=(rows // tm, n_chunks),
        in_specs=specs,
        out_specs=pl.BlockSpec((tm, d), lambda g, c: (g, 0)),
        out_shape=jax.ShapeDtypeStruct((rows, d), F32),
        scratch_shapes=[pltpu.VMEM((tm, d), BF16), _chunk_major(tm, d), _chunk_major(tm, d), pltpu.VMEM((tm, d), F32),
                        pltpu.VMEM((n_pairs, tm // nbs, d_head), F32), pltpu.VMEM((n_pairs, n_mem, d_head), BF16),
                        pltpu.VMEM((n_pairs, n_mem, d_head), BF16)],
        compiler_params=_params(2),
        name=f"xattn_sample_{l}",
    )(*args)


def _memory_kv(mem, p, n_seq, n_mem, d_head):
    d = mem.shape[-1]
    depth = p["w_k"].shape[0]
    r_rows = n_mem * d // LANES
    r_spec = pl.BlockSpec((None, None, r_rows, LANES), lambda l, b: (l, b, 0, 0))
    kt_spec = pl.BlockSpec((None, None, N_HEADS, d_head, n_mem), lambda l, b: (l, b, 0, 0, 0))
    vb_spec = pl.BlockSpec((None, None, N_HEADS, n_mem, d_head), lambda l, b: (l, b, 0, 0, 0))
    w_spec = pl.BlockSpec((None, d, d), lambda l, b: (l, 0, 0))
    r_shape = jax.ShapeDtypeStruct((depth, n_seq, r_rows, LANES), F32)
    return pl.pallas_call(
        _memory_kv_kernel,
        grid=(depth, n_seq),
        in_specs=[pl.BlockSpec((n_mem, d), lambda l, b: (b, 0)), pl.BlockSpec((None, 1, d), lambda l, b: (l, 0, 0)),
                  w_spec, w_spec],
        out_specs=[r_spec, r_spec, kt_spec, vb_spec],
        out_shape=[r_shape, r_shape, jax.ShapeDtypeStruct((depth, n_seq, N_HEADS, d_head, n_mem), BF16),
                   jax.ShapeDtypeStruct((depth, n_seq, N_HEADS, n_mem, d_head), BF16)],
        scratch_shapes=[pltpu.VMEM((n_mem, d), BF16)],
        compiler_params=_params(2, [False, False, True, True]),
        name="memory_kv",
    )(mem, p["norm_mem"], p["w_k"], p["w_v"])


def _to_time_major(x, nb):
    b, t, c = x.shape
    return x.reshape(b // nb, nb, t, c).transpose(0, 2, 1, 3).reshape(b * t, c)


def _from_time_major(y, b, t, nb):
    c = y.shape[-1]
    return y.reshape(b // nb, t, nb, c).transpose(0, 2, 1, 3).reshape(b, t, c)


def _cache_rows(c):
    *lead, n_mem, heads, d_head = c.shape
    chunks = d_head // LANES
    r = c.reshape(*lead, n_mem, heads, chunks, LANES)
    return jnp.swapaxes(r, -3, -2).reshape(*lead, n_mem * chunks * heads, LANES)


def _cache_from_rows(r, n_mem, heads, d_head):
    *lead, _, _ = r.shape
    chunks = d_head // LANES
    c = r.reshape(*lead, n_mem, chunks, heads, LANES)
    return jnp.swapaxes(c, -3, -2).reshape(*lead, n_mem, heads, d_head)


def kernel(x_prompt, x_sample, state_pool, state_conv, state_short, cache_mem_k, cache_mem_v, mem_prompt, norm_mix, w_in_even, pool_w, pool_scale, conv_w, conv_b, ln_g, ln_b, w_out_even, w_in_odd, short_w, w_out_odd, norm_x, norm_mem, w_q, w_k, w_v, w_o, final_norm):
    b_p, t_p, d = x_prompt.shape
    b_s, t_s, _ = x_sample.shape
    depth = norm_mix.shape[0]
    n_mem = mem_prompt.shape[1]
    heads, d_head = cache_mem_k.shape[-2:]
    assert heads == N_HEADS and d_head % LANES == 0 and depth >= 2
    assert d_head & (d_head - 1) == 0 and d_head.bit_length() % 2 == 1, "the score scale must be a power of two"

    nb_p, tt_p, tt_a = b_p, TILE_ROWS // b_p, ATTN_TILE_ROWS // b_p
    nb_s, tt_s = SAMPLE_TILE_ROWS // t_s, t_s
    for nb in (nb_p, nb_s):
        assert nb % SUBLANES == 0 and nb & (nb - 1) == 0
    assert t_p % tt_p == 0 and t_p % tt_a == 0 and b_s % nb_s == 0 and nb_s % XATTN_SEQS == 0

    row = lambda a: a.reshape(a.shape[0], 1, a.shape[-1])
    p = {
        "norm_mix": row(norm_mix), "norm_x": row(norm_x), "norm_mem": row(norm_mem),
        "w_in_even": w_in_even.astype(BF16), "pool_w": pool_w.astype(BF16), "pool_scale": row(pool_scale),
        "conv_w": conv_w, "conv_b": row(conv_b), "ln_g": row(ln_g), "ln_b": row(ln_b),
        "w_out_even": w_out_even.astype(BF16),
        "w_in_odd": w_in_odd.astype(BF16), "short_w": short_w, "w_out_odd": w_out_odd.astype(BF16),
        "w_q": w_q.astype(BF16), "w_k": w_k.astype(BF16), "w_v": w_v.astype(BF16), "w_o": w_o.astype(BF16),
    }
    fnorm = final_norm.reshape(1, d)
    w_a, w_b, w_c = pool_scale.shape[-1], conv_b.shape[-1], short_w.shape[-1]

    kr, vr, kt, vb = _memory_kv(mem_prompt.reshape(b_p * n_mem, d), p, b_p, n_mem, d_head)
    mem_k_p = _cache_from_rows(kr, n_mem, heads, d_head)
    mem_v_p = _cache_from_rows(vr, n_mem, heads, d_head)
    kt = kt.reshape(depth, b_p * heads, d_head, n_mem)
    vb = vb.reshape(depth, b_p * heads, n_mem, d_head)

    xp = x_prompt
    zp = jnp.zeros((POOL_HIST * nb_p, w_a), F32)
    zc = jnp.zeros((CONV_HIST * nb_p, w_b), F32)
    zs = jnp.zeros((SHORT_HIST * nb_p, w_c), F32)
    pool_p, conv_p, short_p = [], [], []
    for l in range(depth):
        last = l == depth - 1
        if l % 2 == 0:
            xp, sp, sc = _even_layer(xp, zp, zc, p, l // 2, l, nb=nb_p, tt=tt_p, start_pos=0, carry=True)
            pool_p.append(_from_time_major(sp, b_p, POOL_HIST, nb_p))
            conv_p.append(_from_time_major(sc, b_p, CONV_HIST, nb_p))
        else:
            xp, ss = _odd_layer(xp, zs, p, l // 2, l, nb=nb_p, tt=tt_p, carry=True)
            short_p.append(_from_time_major(ss, b_p, SHORT_HIST, nb_p))
        xp = _xattn_prompt(xp, kt, vb, p, l, nb=nb_p, tt=tt_a, final_norm=fnorm if last else None,
                           batch_major_out=last)
    y_prompt = xp

    krs, vrs = _cache_rows(cache_mem_k), _cache_rows(cache_mem_v)
    xs = _to_time_major(x_sample, nb_s)
    pool_s, conv_s, short_s = [], [], []
    for l in range(depth):
        if l % 2 == 0:
            e = l // 2
            xs, sp, sc = _even_layer(xs, _to_time_major(state_pool[e], nb_s), _to_time_major(state_conv[e], nb_s),
                                     p, e, l, nb=nb_s, tt=tt_s, start_pos=PAST_LEN, carry=False)
            pool_s.append(_from_time_major(sp, b_s, POOL_HIST, nb_s))
            conv_s.append(_from_time_major(sc, b_s, CONV_HIST, nb_s))
        else:
            o = l // 2
            xs, ss = _odd_layer(xs, _to_time_major(state_short[o], nb_s), p, o, l, nb=nb_s, tt=tt_s, carry=False)
            short_s.append(_from_time_major(ss, b_s, SHORT_HIST, nb_s))
        xs = _xattn_sample(xs, krs, vrs, p, l, nbs=nb_s, tm=nb_s * tt_s,
                           final_norm=fnorm if l == depth - 1 else None)
    y_sample = _from_time_major(xs, b_s, t_s, nb_s)

    return (y_prompt, y_sample, jnp.stack(pool_p), jnp.stack(conv_p), jnp.stack(short_p), mem_k_p, mem_v_p,
            jnp.stack(pool_s), jnp.stack(conv_s), jnp.stack(short_s))
```

```python
import functools

import jax
import jax.numpy as jnp
from jax import lax
from jax.experimental import pallas as pl
from jax.experimental.pallas import tpu as pltpu

F32 = jnp.float32
BF16 = jnp.bfloat16

EPS = 1e-6
POOL_WINDOWS = (2, 4, 8, 16)
POOL_HIST = max(POOL_WINDOWS) - 1
CONV_TAPS = 31
CONV_HIST = CONV_TAPS - 1
SHORT_TAPS = 3
SHORT_HIST = SHORT_TAPS - 1
N_HEADS = 4
PAST_LEN = 16384

LANES = 128
SUBLANES = 8
MXU_COLS = 256
VMEM_LIMIT_BYTES = 56 * 1024 * 1024

TILE_ROWS = 512
SAMPLE_TILE_ROWS = 256
ATTN_TILE_ROWS = 1024
NORM_ROWS = 32
CONV_ROWS = 128
CONV_TAP_GROUP = 8
XATTN_SEQS = 8


def _mm(a, b):
    return jnp.dot(a, b, preferred_element_type=F32)


def _silu(x):
    return x * jax.nn.sigmoid(x)


def _rmsnorm(x, g):
    ms = jnp.mean(x * x, axis=-1, keepdims=True)
    return x * lax.rsqrt(ms + EPS) * g


def _softmax(s):
    e = jnp.exp(s - jnp.max(s, axis=-1, keepdims=True))
    return e / jnp.sum(e, axis=-1, keepdims=True)


def _norm_rows(dst_ref, src_ref, g_ref, dtype):
    rows = src_ref.shape[0]
    for r0 in range(0, rows, NORM_ROWS):
        rs = slice(r0, r0 + NORM_ROWS)
        dst_ref[rs, :] = _rmsnorm(src_ref[rs, :], g_ref[...]).astype(dtype)


def _window_rows(hist_ref, new_ref, start, n_rows, cols):
    hist_rows = hist_ref.shape[0]
    n_hist = min(max(hist_rows - start, 0), n_rows)
    parts = []
    if n_hist > 0:
        parts.append(hist_ref[start:start + n_hist, cols])
    if n_hist < n_rows:
        s = start + n_hist - hist_rows
        parts.append(new_ref[s:s + n_rows - n_hist, cols])
    return parts[0] if len(parts) == 1 else jnp.concatenate(parts, axis=0)


def _last_rows(dst_ref, hist_ref, new_ref):
    n, tm = dst_ref.shape[0], new_ref.shape[0]
    if tm >= n:
        dst_ref[...] = new_ref[tm - n:tm, :]
    else:
        dst_ref[0:n - tm, :] = hist_ref[tm:n, :]
        dst_ref[n - tm:n, :] = new_ref[...]


def _exact_zero_of(x):
    bits = lax.bitcast_convert_type(x, jnp.uint32)
    sixteen = jnp.full(bits.shape, 16, jnp.uint32)
    bits = lax.shift_right_logical(lax.shift_right_logical(bits, sixteen), sixteen)
    return lax.bitcast_convert_type(bits, F32)


def _lane_chunks(width):
    return [slice(c0, c0 + LANES) for c0 in range(0, width, LANES)]


def _even_kernel(*refs, nb, tt, start_pos, carry, batch_major_in):
    tm = nb * tt
    it = iter(refs)
    x_ref, hp_ref, hc_ref, nmix_ref, win_ref, poolw_ref, pscale_ref = (next(it) for _ in range(7))
    convw_ref, convb_ref, lng_ref, lnb_ref, wout_ref = (next(it) for _ in range(5))
    o_ref, newp_ref, newc_ref = next(it), next(it), next(it)
    h_scr, pool_scr, vnew_scr, ag_scr, bg_scr, c_scr, ac_scr, wtile_scr = (next(it) for _ in range(8))
    hc_scr = next(it) if carry else None
    if batch_major_in:
        xc_scr, xt_scr = next(it), next(it)

    i = pl.program_id(0)
    n_steps = pl.num_programs(0)
    w_a = pscale_ref.shape[-1]
    w_b = convb_ref.shape[-1]
    hp_rows = POOL_HIST * nb
    pool_group = w_a // len(POOL_WINDOWS)

    if carry:
        @pl.when(i == 0)
        def _():
            pool_scr[0:hp_rows, :] = hp_ref[...]
            hc_scr[...] = hc_ref[...]
        hist_c = hc_scr
    else:
        pool_scr[0:hp_rows, :] = hp_ref[...]
        hist_c = hc_ref

    if batch_major_in:
        for b in range(nb):
            for j, cs in enumerate(_lane_chunks(x_ref.shape[-1])):
                xc_scr[j, pl.ds(b, tt, stride=nb), :] = x_ref[b, :, cs]
        for j, cs in enumerate(_lane_chunks(x_ref.shape[-1])):
            xt_scr[:, cs] = xc_scr[j]
        x_src = xt_scr
    else:
        x_src = x_ref

    _norm_rows(h_scr, x_src, nmix_ref, BF16)

    def column_piece(col0, c0):
        return _mm(h_scr[...], win_ref[:, col0 + c0:col0 + c0 + MXU_COLS])
    for c0 in range(0, w_b, MXU_COLS):
        vnew_scr[:, c0:c0 + MXU_COLS] = column_piece(2 * w_a, c0) * jax.nn.sigmoid(column_piece(2 * w_a + w_b, c0))

    def project(dst_ref, dst_rows, col0, c0):
        def run():
            dst_ref[dst_rows, c0:c0 + MXU_COLS] = column_piece(col0, c0)
        return run
    pieces = ([project(pool_scr, slice(hp_rows, hp_rows + tm), 0, c0) for c0 in range(0, w_a, MXU_COLS)]
              + [project(ag_scr, slice(None), w_a, c0) for c0 in range(0, w_a, MXU_COLS)]
              + [project(bg_scr, slice(None), 2 * w_a + 2 * w_b, c0) for c0 in range(0, w_b, MXU_COLS)])
    conv_chunks = _lane_chunks(w_b)
    for k in range(CONV_TAPS):
        wtile_scr[k * SUBLANES:(k + 1) * SUBLANES, :] = jnp.broadcast_to(convw_ref[k:k + 1, :], (SUBLANES, w_b))

    def conv_chunk(n, cs):
        order = None
        for r0 in range(0, tm, CONV_ROWS):
            bias = jnp.broadcast_to(convb_ref[:, cs], (SUBLANES, LANES))
            if order is not None:
                bias = bias + order
            acc = jnp.concatenate([bias] * (CONV_ROWS // SUBLANES), axis=0)
            for k0 in range(0, CONV_TAPS, CONV_TAP_GROUP):
                taps = range(k0, min(k0 + CONV_TAP_GROUP, CONV_TAPS))
                win = _window_rows(hist_c, vnew_scr, r0 + k0 * nb, (len(taps) - 1) * nb + CONV_ROWS, cs)
                for k in taps:
                    wk = wtile_scr[k * SUBLANES:(k + 1) * SUBLANES, cs]
                    acc = acc + (win[(k - k0) * nb:(k - k0) * nb + CONV_ROWS, :]
                                 * jnp.concatenate([wk] * (CONV_ROWS // SUBLANES), axis=0))
            c_scr[r0:r0 + CONV_ROWS, cs] = acc
            order = _exact_zero_of(acc[0:SUBLANES, :])
        for run in pieces[n * len(pieces) // len(conv_chunks):(n + 1) * len(pieces) // len(conv_chunks)]:
            run()

    for n, cs in enumerate(conv_chunks):
        pl.when(i >= 0)(functools.partial(conv_chunk, n, cs))

    row = lax.broadcasted_iota(jnp.int32, (tm, LANES), 0)
    step = row >> (nb.bit_length() - 1)
    pos1 = step + (start_pos + 1 + (i * tt if carry else 0))
    for g, w in enumerate(POOL_WINDOWS):
        inv_cnt = 1.0 / jnp.minimum(pos1, w).astype(F32)
        a_parts = []
        for c0 in range(g * pool_group, (g + 1) * pool_group, LANES):
            e = pool_scr[(POOL_HIST + 1 - w) * nb:hp_rows + tm, c0:c0 + LANES]
            s, sh = e, 1
            while sh < w:
                n = s.shape[0]
                s = s[sh * nb:, :] + s[:n - sh * nb, :]
                sh *= 2
            a_parts.append((s * inv_cnt - e[(w - 1) * nb:, :]).astype(BF16))
        cs = slice(g * pool_group, (g + 1) * pool_group)
        pa = _mm(jnp.concatenate(a_parts, axis=1), poolw_ref[g])
        ac_scr[:, cs] = (pa * pscale_ref[:, cs] * _silu(ag_scr[:, cs])).astype(BF16)

    o_ref[...] = x_src[...] + _mm(ac_scr[:, 0:w_a], wout_ref[0:w_a, :])

    for r0 in range(0, tm, NORM_ROWS):
        rs = slice(r0, r0 + NORM_ROWS)
        c = c_scr[rs, :]
        d = c - jnp.mean(c, axis=-1, keepdims=True)
        var = jnp.mean(d * d, axis=-1, keepdims=True)
        y = d * lax.rsqrt(var + EPS) * lng_ref[...] + lnb_ref[...]
        ac_scr[rs, w_a:w_a + w_b] = (_silu(y) * _silu(bg_scr[rs, :])).astype(BF16)

    o_ref[...] += _mm(ac_scr[:, w_a:w_a + w_b], wout_ref[w_a:w_a + w_b, :])

    if carry:
        _last_rows(hc_scr, hc_scr, vnew_scr)
        hp_new = pool_scr[tm:tm + hp_rows, :]
        pool_scr[0:hp_rows, :] = hp_new

        @pl.when(i == n_steps - 1)
        def _():
            newp_ref[...] = pool_scr[0:hp_rows, :]
            newc_ref[...] = hc_scr[...]
    else:
        newp_ref[...] = pool_scr[tm:tm + hp_rows, :]
        _last_rows(newc_ref, hc_ref, vnew_scr)


def _odd_kernel(*refs, nb, tt, carry):
    tm = nb * tt
    it = iter(refs)
    x_ref, hs_ref, nmix_ref, win_ref, shortw_ref, wout_ref = (next(it) for _ in range(6))
    o_ref, news_ref = next(it), next(it)
    h_scr, unew_scr, gate_scr, y_scr = (next(it) for _ in range(4))
    hs_scr = next(it) if carry else None

    i = pl.program_id(0)
    n_steps = pl.num_programs(0)
    w_c = shortw_ref.shape[-1]
    d_model = x_ref.shape[-1]

    if carry:
        @pl.when(i == 0)
        def _():
            hs_scr[...] = hs_ref[...]
        hist = hs_scr
    else:
        hist = hs_ref

    _norm_rows(h_scr, x_ref, nmix_ref, BF16)

    for c0 in range(0, w_c, d_model):
        cs = slice(c0, c0 + d_model)
        gc = _mm(h_scr[...], win_ref[:, w_c + c0:w_c + c0 + d_model])
        xv = _mm(h_scr[...], win_ref[:, 2 * w_c + c0:2 * w_c + c0 + d_model])
        unew_scr[:, cs] = gc * xv
    for c0 in range(0, w_c, d_model):
        cs = slice(c0, c0 + d_model)
        gb = _mm(h_scr[...], win_ref[:, c0:c0 + d_model])
        gg = _mm(h_scr[...], win_ref[:, 3 * w_c + c0:3 * w_c + c0 + d_model])
        gate_scr[:, cs] = gb * _silu(gg)
    for cs in _lane_chunks(w_c):
        acc = _window_rows(hist, unew_scr, 0, tm, cs) * shortw_ref[0:1, cs]
        for k in range(1, SHORT_TAPS):
            acc = acc + _window_rows(hist, unew_scr, k * nb, tm, cs) * shortw_ref[k:k + 1, cs]
        y_scr[:, cs] = (gate_scr[:, cs] * acc).astype(BF16)

    o_ref[...] = x_ref[...] + _mm(y_scr[...], wout_ref[...])

    if carry:
        _last_rows(hs_scr, hs_scr, unew_scr)

        @pl.when(i == n_steps - 1)
        def _():
            news_ref[...] = hs_scr[...]
    else:
        _last_rows(news_ref, hs_ref, unew_scr)


def _project_q(h_scr, wq_ref, q_scr, d_head):
    chunks = d_head // LANES
    for hd in range(N_HEADS):
        q = _mm(h_scr[...], wq_ref[:, hd * d_head:(hd + 1) * d_head]) * d_head ** -0.5
        for c in range(chunks):
            q_scr[hd * chunks + c] = q[:, c * LANES:(c + 1) * LANES]


def _pair_slot(seq, j, chunks):
    return seq * N_HEADS + j // chunks, slice((j % chunks) * LANES, (j % chunks + 1) * LANES)


def _xattn_prompt_kernel(*refs, nb, tt, final, batch_major_out):
    it = iter(refs)
    x_ref, nx_ref, wq_ref, kt_ref, v_ref, wo_ref = (next(it) for _ in range(6))
    fn_ref = next(it) if final else None
    o_ref = next(it)
    h_scr, q_scr, oh_scr, qg_scr, y_scr = (next(it) for _ in range(5))
    d_head = v_ref.shape[-1]
    chunks = d_head // LANES
    n_chunks = N_HEADS * chunks

    _norm_rows(h_scr, x_ref, nx_ref, BF16)
    _project_q(h_scr, wq_ref, q_scr, d_head)
    for b in range(nb):
        for j in range(n_chunks):
            pair, cs = _pair_slot(b, j, chunks)
            qg_scr[pair, :, cs] = q_scr[j, pl.ds(b, tt, stride=nb), :].astype(BF16)
    s = jnp.einsum("pqd,pdk->pqk", qg_scr[...], kt_ref[...], preferred_element_type=F32)
    o = jnp.einsum("pqk,pkd->pqd", _softmax(s).astype(BF16), v_ref[...], preferred_element_type=F32)
    for b in range(nb):
        for j in range(n_chunks):
            pair, cs = _pair_slot(b, j, chunks)
            oh_scr[j, pl.ds(b, tt, stride=nb), :] = o[pair, :, cs]
    o_all = jnp.concatenate([oh_scr[j] for j in range(n_chunks)], axis=1)
    out = x_ref[...] + _mm(o_all.astype(BF16), wo_ref[...])

    if not (final or batch_major_out):
        o_ref[...] = out
        return
    y_scr[...] = out
    if final:
        _norm_rows(y_scr, y_scr, fn_ref, F32)
    if not batch_major_out:
        o_ref[...] = y_scr[...]
        return
    for j, cs in enumerate(_lane_chunks(y_scr.shape[-1])):
        q_scr[j] = y_scr[:, cs]
    for b in range(nb):
        for j, cs in enumerate(_lane_chunks(y_scr.shape[-1])):
            o_ref[b, :, cs] = q_scr[j, pl.ds(b, tt, stride=nb), :]


def _xattn_sample_kernel(*refs, nbs, final):
    it = iter(refs)
    x_ref, nx_ref, wq_ref, k_ref, v_ref, wo_ref = (next(it) for _ in range(6))
    fn_ref = next(it) if final else None
    o_ref = next(it)
    h_scr, q_scr, oh_scr, y_scr, qg_scr, kg_scr, vg_scr = (next(it) for _ in range(7))

    c = pl.program_id(1)
    tt = x_ref.shape[0] // nbs
    seqs = k_ref.shape[0]
    d_head = wq_ref.shape[-1] // N_HEADS
    chunks = d_head // LANES
    n_chunks = N_HEADS * chunks
    n_mem = k_ref.shape[1] // n_chunks

    @pl.when(c == 0)
    def _():
        _norm_rows(h_scr, x_ref, nx_ref, BF16)
        _project_q(h_scr, wq_ref, q_scr, d_head)

    for bl in range(seqs):
        rows = pl.ds(c * seqs + bl, tt, stride=nbs)
        for j in range(n_chunks):
            pair, cs = _pair_slot(bl, j, chunks)
            cache_rows = pl.ds((j % chunks) * N_HEADS + j // chunks, n_mem, stride=n_chunks)
            qg_scr[pair, :, cs] = q_scr[j, rows, :]
            kg_scr[pair, :, cs] = k_ref[bl, cache_rows, :].astype(BF16)
            vg_scr[pair, :, cs] = v_ref[bl, cache_rows, :].astype(BF16)
    s = jnp.einsum("pqd,pkd->pqk", qg_scr[...].astype(BF16), kg_scr[...], preferred_element_type=F32)
    o = jnp.einsum("pqk,pkd->pqd", _softmax(s).astype(BF16), vg_scr[...], preferred_element_type=F32)
    for bl in range(seqs):
        rows = pl.ds(c * seqs + bl, tt, stride=nbs)
        for j in range(n_chunks):
            pair, cs = _pair_slot(bl, j, chunks)
            oh_scr[j, rows, :] = o[pair, :, cs]

    @pl.when(c == pl.num_programs(1) - 1)
    def _():
        o_all = jnp.concatenate([oh_scr[j] for j in range(n_chunks)], axis=1)
        out = x_ref[...] + _mm(o_all.astype(BF16), wo_ref[...])
        if final:
            y_scr[...] = out
            _norm_rows(o_ref, y_scr, fn_ref, F32)
        else:
            o_ref[...] = out


def _memory_kv_kernel(*refs, n_casts):
    mem_ref, nmem_ref, wk_ref, wv_ref = refs[:4]
    cast_in = refs[4:4 + n_casts]
    kr_ref, vr_ref, kt_ref, vb_ref = refs[4 + n_casts:8 + n_casts]
    cast_out = refs[8 + n_casts:8 + 2 * n_casts]
    h_scr = refs[8 + 2 * n_casts]
    for src, dst in zip(cast_in, cast_out):
        dst[...] = src[...].astype(BF16)
    n_mem = mem_ref.shape[0]
    d_head = vb_ref.shape[-1]
    chunks = d_head // LANES
    n_chunks = chunks * N_HEADS
    _norm_rows(h_scr, mem_ref, nmem_ref, BF16)
    for w_ref, r_ref in ((wk_ref, kr_ref), (wv_ref, vr_ref)):
        for hd in range(N_HEADS):
            kh = _mm(h_scr[...], w_ref[:, hd * d_head:(hd + 1) * d_head])
            if r_ref is kr_ref:
                kt_ref[hd] = kh.T.astype(BF16)
            else:
                vb_ref[hd] = kh.astype(BF16)
            for cc in range(chunks):
                r_ref[pl.ds(cc * N_HEADS + hd, n_mem, stride=n_chunks), :] = kh[:, cc * LANES:(cc + 1) * LANES]


def _const_spec(shape, layer=None):
    if layer is None:
        return pl.BlockSpec(shape, lambda *_: (0,) * len(shape), pipeline_mode=pl.Buffered(1))
    return pl.BlockSpec((None,) + shape, lambda *_: (layer,) + (0,) * len(shape), pipeline_mode=pl.Buffered(1))


def _row_spec(rows, cols):
    return pl.BlockSpec((rows, cols), lambda i: (i, 0))


def _params(n_axes, input_fusion=None):
    return pltpu.CompilerParams(dimension_semantics=("arbitrary",) * n_axes, vmem_limit_bytes=VMEM_LIMIT_BYTES,
                                allow_input_fusion=input_fusion)


def _chunk_major(tm, d):
    return pltpu.VMEM((d // LANES, tm, LANES), F32)


def _even_layer(x, hp, hc, p, e, l, *, nb, tt, start_pos, carry):
    batch_major_in = x.ndim == 3
    d = x.shape[-1]
    tm = nb * tt
    rows = x.shape[0] * x.shape[1] if batch_major_in else x.shape[0]
    w_a, w_b = p["pool_scale"].shape[-1], p["conv_b"].shape[-1]
    hp_rows, hc_rows = POOL_HIST * nb, CONV_HIST * nb
    assert not carry or tm >= hc_rows
    assert not batch_major_in or (carry and x.shape[0] == nb)
    n_groups = len(POOL_WINDOWS)

    args = [x, hp, hc, p["norm_mix"], p["w_in_even"], p["pool_w"], p["pool_scale"], p["conv_w"], p["conv_b"],
            p["ln_g"], p["ln_b"], p["w_out_even"]]
    hist_spec = (lambda r, c: _const_spec((r, c))) if carry else _row_spec
    x_spec = pl.BlockSpec((nb, tt, d), lambda i: (0, i, 0)) if batch_major_in else _row_spec(tm, d)
    specs = [x_spec, hist_spec(hp_rows, w_a), hist_spec(hc_rows, w_b),
             _const_spec((1, d), l), _const_spec((d, 2 * w_a + 3 * w_b), e),
             _const_spec((n_groups, w_a // n_groups, w_a // n_groups), e), _const_spec((1, w_a), e),
             _const_spec((CONV_TAPS, w_b), e), _const_spec((1, w_b), e), _const_spec((1, w_b), e),
             _const_spec((1, w_b), e), _const_spec((w_a + w_b, d), e)]

    state_spec = (lambda r, c: pl.BlockSpec((r, c), lambda i: (0, 0))) if carry else _row_spec
    n_state = 1 if carry else rows // tm
    scratch = [pltpu.VMEM((tm, d), BF16), pltpu.VMEM((hp_rows + tm, w_a), F32), pltpu.VMEM((tm, w_b), F32),
               pltpu.VMEM((tm, w_a), F32), pltpu.VMEM((tm, w_b), F32), pltpu.VMEM((tm, w_b), F32),
               pltpu.VMEM((tm, w_a + w_b), BF16), pltpu.VMEM((CONV_TAPS * SUBLANES, w_b), F32)]
    if carry:
        scratch.append(pltpu.VMEM((hc_rows, w_b), F32))
    if batch_major_in:
        scratch += [_chunk_major(tm, d), pltpu.VMEM((tm, d), F32)]
    return pl.pallas_call(
        functools.partial(_even_kernel, nb=nb, tt=tt, start_pos=start_pos, carry=carry,
                          batch_major_in=batch_major_in),
        grid=(rows // tm,),
        in_specs=specs,
        out_specs=[_row_spec(tm, d), state_spec(hp_rows, w_a), state_spec(hc_rows, w_b)],
        out_shape=[jax.ShapeDtypeStruct((rows, d), F32), jax.ShapeDtypeStruct((n_state * hp_rows, w_a), F32),
                   jax.ShapeDtypeStruct((n_state * hc_rows, w_b), F32)],
        scratch_shapes=scratch,
        compiler_params=_params(1),
        name=f"even_mixer_{l}_{'carry' if carry else 'group'}",
    )(*args)


def _odd_layer(x, hs, p, o, l, *, nb, tt, carry):
    rows, d = x.shape
    tm = nb * tt
    w_c = p["short_w"].shape[-1]
    hs_rows = SHORT_HIST * nb
    assert not carry or tm >= hs_rows

    args = [x, hs, p["norm_mix"], p["w_in_odd"], p["short_w"], p["w_out_odd"]]
    hist_spec = (lambda r, c: _const_spec((r, c))) if carry else _row_spec
    specs = [_row_spec(tm, d), hist_spec(hs_rows, w_c), _const_spec((1, d), l), _const_spec((d, 4 * w_c), o),
             _const_spec((SHORT_TAPS, w_c), o), _const_spec((w_c, d), o)]
    state_spec = (lambda r, c: pl.BlockSpec((r, c), lambda i: (0, 0))) if carry else _row_spec
    n_state = 1 if carry else rows // tm
    scratch = [pltpu.VMEM((tm, d), BF16), pltpu.VMEM((tm, w_c), F32), pltpu.VMEM((tm, w_c), F32),
               pltpu.VMEM((tm, w_c), BF16)]
    if carry:
        scratch.append(pltpu.VMEM((hs_rows, w_c), F32))
    return pl.pallas_call(
        functools.partial(_odd_kernel, nb=nb, tt=tt, carry=carry),
        grid=(rows // tm,),
        in_specs=specs,
        out_specs=[_row_spec(tm, d), state_spec(hs_rows, w_c)],
        out_shape=[jax.ShapeDtypeStruct((rows, d), F32), jax.ShapeDtypeStruct((n_state * hs_rows, w_c), F32)],
        scratch_shapes=scratch,
        compiler_params=_params(1),
        name=f"odd_mixer_{l}_{'carry' if carry else 'group'}",
    )(*args)


def _xattn_prompt(x, kt, vb, p, l, *, nb, tt, final_norm=None, batch_major_out=False):
    rows, d = x.shape
    tm = nb * tt
    n_pairs, d_head = nb * N_HEADS, d // N_HEADS
    assert kt.shape[1] == n_pairs and vb.shape[1] == n_pairs
    final = final_norm is not None
    args = [x, p["norm_x"], p["w_q"], kt, vb, p["w_o"]]
    specs = [_row_spec(tm, d), _const_spec((1, d), l), _const_spec((d, d), l), _const_spec(kt.shape[1:], l),
             _const_spec(vb.shape[1:], l), _const_spec((d, d), l)]
    if final:
        args, specs = args + [final_norm], specs + [_const_spec((1, d))]
    if batch_major_out:
        out_spec = pl.BlockSpec((nb, tt, d), lambda i: (0, i, 0))
        out_shape = jax.ShapeDtypeStruct((nb, rows // nb, d), F32)
    else:
        out_spec, out_shape = _row_spec(tm, d), jax.ShapeDtypeStruct((rows, d), F32)
    return pl.pallas_call(
        functools.partial(_xattn_prompt_kernel, nb=nb, tt=tt, final=final, batch_major_out=batch_major_out),
        grid=(rows // tm,),
        in_specs=specs,
        out_specs=out_spec,
        out_shape=out_shape,
        scratch_shapes=[pltpu.VMEM((tm, d), BF16), _chunk_major(tm, d), _chunk_major(tm, d),
                        pltpu.VMEM((n_pairs, tt, d_head), BF16), pltpu.VMEM((tm, d), F32)],
        compiler_params=_params(1),
        name=f"xattn_prompt_{l}",
    )(*args)


def _xattn_sample(x, kr, vr, p, l, *, nbs, tm, final_norm=None):
    rows, d = x.shape
    n_chunks = nbs // XATTN_SEQS
    final = final_norm is not None
    d_head = d // N_HEADS
    n_pairs = XATTN_SEQS * N_HEADS
    n_mem = kr.shape[2] * LANES // d
    cache_block = (XATTN_SEQS,) + kr.shape[2:]
    cache_spec = pl.BlockSpec((None,) + cache_block, lambda g, c: (l, g * n_chunks + c, 0, 0))
    args = [x, p["norm_x"], p["w_q"], kr, vr, p["w_o"]]
    specs = [pl.BlockSpec((tm, d), lambda g, c: (g, 0)), _const_spec((1, d), l), _const_spec((d, d), l),
             cache_spec, cache_spec, _const_spec((d, d), l)]
    if final:
        args, specs = args + [final_norm], specs + [_const_spec((1, d))]
    return pl.pallas_call(
        functools.partial(_xattn_sample_kernel, nbs=nbs, final=final),
        grid=(rows // tm, n_chunks),
        in_specs=specs,
        out_specs=pl.BlockSpec((tm, d), lambda g, c: (g, 0)),
        out_shape=jax.ShapeDtypeStruct((rows, d), F32),
        scratch_shapes=[pltpu.VMEM((tm, d), BF16), _chunk_major(tm, d), _chunk_major(tm, d), pltpu.VMEM((tm, d), F32),
                        pltpu.VMEM((n_pairs, tm // nbs, d_head), F32), pltpu.VMEM((n_pairs, n_mem, d_head), BF16),
                        pltpu.VMEM((n_pairs, n_mem, d_head), BF16)],
        compiler_params=_params(2),
        name=f"xattn_sample_{l}",
    )(*args)


def _memory_kv(mem, p, casts, n_seq, n_mem, d_head):
    d = mem.shape[-1]
    depth = p["w_k"].shape[0]
    n_steps = depth * n_seq
    cast_specs = []
    for w in casts:
        assert w.shape[0] % (n_steps * 2 * SUBLANES) == 0
        cast_specs.append(pl.BlockSpec((w.shape[0] // n_steps, w.shape[1]), lambda l, b: (l * n_seq + b, 0)))
    r_rows = n_mem * d // LANES
    r_spec = pl.BlockSpec((None, None, r_rows, LANES), lambda l, b: (l, b, 0, 0))
    kt_spec = pl.BlockSpec((None, None, N_HEADS, d_head, n_mem), lambda l, b: (l, b, 0, 0, 0))
    vb_spec = pl.BlockSpec((None, None, N_HEADS, n_mem, d_head), lambda l, b: (l, b, 0, 0, 0))
    w_spec = pl.BlockSpec((None, d, d), lambda l, b: (l, 0, 0))
    r_shape = jax.ShapeDtypeStruct((depth, n_seq, r_rows, LANES), F32)
    return pl.pallas_call(
        functools.partial(_memory_kv_kernel, n_casts=len(casts)),
        grid=(depth, n_seq),
        in_specs=[pl.BlockSpec((n_mem, d), lambda l, b: (b, 0)), pl.BlockSpec((None, 1, d), lambda l, b: (l, 0, 0)),
                  w_spec, w_spec] + cast_specs,
        out_specs=[r_spec, r_spec, kt_spec, vb_spec] + cast_specs,
        out_shape=[r_shape, r_shape, jax.ShapeDtypeStruct((depth, n_seq, N_HEADS, d_head, n_mem), BF16),
                   jax.ShapeDtypeStruct((depth, n_seq, N_HEADS, n_mem, d_head), BF16)]
        + [jax.ShapeDtypeStruct(w.shape, BF16) for w in casts],
        scratch_shapes=[pltpu.VMEM((n_mem, d), BF16)],
        compiler_params=_params(2, [False, False, True, True] + [False] * len(casts)),
        name="memory_kv",
    )(mem, p["norm_mem"], p["w_k"], p["w_v"], *casts)


def _to_time_major(x, nb):
    b, t, c = x.shape
    return x.reshape(b // nb, nb, t, c).transpose(0, 2, 1, 3).reshape(b * t, c)


def _from_time_major(y, b, t, nb):
    c = y.shape[-1]
    return y.reshape(b // nb, t, nb, c).transpose(0, 2, 1, 3).reshape(b, t, c)


def _cache_rows(c):
    *lead, n_mem, heads, d_head = c.shape
    chunks = d_head // LANES
    r = c.reshape(*lead, n_mem, heads, chunks, LANES)
    return jnp.swapaxes(r, -3, -2).reshape(*lead, n_mem * chunks * heads, LANES)


def _cache_from_rows(r, n_mem, heads, d_head):
    *lead, _, _ = r.shape
    chunks = d_head // LANES
    c = r.reshape(*lead, n_mem, chunks, heads, LANES)
    return jnp.swapaxes(c, -3, -2).reshape(*lead, n_mem, heads, d_head)


def kernel(x_prompt, x_sample, state_pool, state_conv, state_short, cache_mem_k, cache_mem_v, mem_prompt, norm_mix, w_in_even, pool_w, pool_scale, conv_w, conv_b, ln_g, ln_b, w_out_even, w_in_odd, short_w, w_out_odd, norm_x, norm_mem, w_q, w_k, w_v, w_o, final_norm):
    b_p, t_p, d = x_prompt.shape
    b_s, t_s, _ = x_sample.shape
    depth = norm_mix.shape[0]
    n_mem = mem_prompt.shape[1]
    heads, d_head = cache_mem_k.shape[-2:]
    assert heads == N_HEADS and d_head % LANES == 0 and depth >= 2
    assert d_head & (d_head - 1) == 0 and d_head.bit_length() % 2 == 1, "the score scale must be a power of two"

    nb_p, tt_p, tt_a = b_p, TILE_ROWS // b_p, ATTN_TILE_ROWS // b_p
    nb_s, tt_s = SAMPLE_TILE_ROWS // t_s, t_s
    for nb in (nb_p, nb_s):
        assert nb % SUBLANES == 0 and nb & (nb - 1) == 0
    assert t_p % tt_p == 0 and t_p % tt_a == 0 and b_s % nb_s == 0 and nb_s % XATTN_SEQS == 0

    row = lambda a: a.reshape(a.shape[0], 1, a.shape[-1])
    p = {
        "norm_mix": row(norm_mix), "norm_x": row(norm_x), "norm_mem": row(norm_mem),
        "pool_scale": row(pool_scale), "conv_w": conv_w, "conv_b": row(conv_b), "ln_g": row(ln_g), "ln_b": row(ln_b),
        "short_w": short_w, "w_k": w_k.astype(BF16), "w_v": w_v.astype(BF16),
    }
    fnorm = final_norm.reshape(1, d)
    w_a, w_b, w_c = pool_scale.shape[-1], conv_b.shape[-1], short_w.shape[-1]

    trunk = {"w_in_even": w_in_even, "pool_w": pool_w, "w_out_even": w_out_even, "w_in_odd": w_in_odd,
             "w_out_odd": w_out_odd, "w_q": w_q, "w_o": w_o}
    kr, vr, kt, vb, *cast = _memory_kv(mem_prompt.reshape(b_p * n_mem, d), p,
                                       [w.reshape(-1, w.shape[-1]) for w in trunk.values()], b_p, n_mem, d_head)
    p.update({name: c.reshape(w.shape) for (name, w), c in zip(trunk.items(), cast)})
    mem_k_p = _cache_from_rows(kr, n_mem, heads, d_head)
    mem_v_p = _cache_from_rows(vr, n_mem, heads, d_head)
    kt = kt.reshape(depth, b_p * heads, d_head, n_mem)
    vb = vb.reshape(depth, b_p * heads, n_mem, d_head)

    xp = x_prompt
    zp = jnp.zeros((POOL_HIST * nb_p, w_a), F32)
    zc = jnp.zeros((CONV_HIST * nb_p, w_b), F32)
    zs = jnp.zeros((SHORT_HIST * nb_p, w_c), F32)
    pool_p, conv_p, short_p = [], [], []
    for l in range(depth):
        last = l == depth - 1
        if l % 2 == 0:
            xp, sp, sc = _even_layer(xp, zp, zc, p, l // 2, l, nb=nb_p, tt=tt_p, start_pos=0, carry=True)
            pool_p.append(_from_time_major(sp, b_p, POOL_HIST, nb_p))
            conv_p.append(_from_time_major(sc, b_p, CONV_HIST, nb_p))
        else:
            xp, ss = _odd_layer(xp, zs, p, l // 2, l, nb=nb_p, tt=tt_p, carry=True)
            short_p.append(_from_time_major(ss, b_p, SHORT_HIST, nb_p))
        xp = _xattn_prompt(xp, kt, vb, p, l, nb=nb_p, tt=tt_a, final_norm=fnorm if last else None,
                           batch_major_out=last)
    y_prompt = xp

    krs, vrs = _cache_rows(cache_mem_k), _cache_rows(cache_mem_v)
    xs = _to_time_major(x_sample, nb_s)
    pool_s, conv_s, short_s = [], [], []
    for l in range(depth):
        if l % 2 == 0:
            e = l // 2
            xs, sp, sc = _even_layer(xs, _to_time_major(state_pool[e], nb_s), _to_time_major(state_conv[e], nb_s),
                                     p, e, l, nb=nb_s, tt=tt_s, start_pos=PAST_LEN, carry=False)
            pool_s.append(_from_time_major(sp, b_s, POOL_HIST, nb_s))
            conv_s.append(_from_time_major(sc, b_s, CONV_HIST, nb_s))
        else:
            o = l // 2
            xs, ss = _odd_layer(xs, _to_time_major(state_short[o], nb_s), p, o, l, nb=nb_s, tt=tt_s, carry=False)
            short_s.append(_from_time_major(ss, b_s, SHORT_HIST, nb_s))
        xs = _xattn_sample(xs, krs, vrs, p, l, nbs=nb_s, tm=nb_s * tt_s,
                           final_norm=fnorm if l == depth - 1 else None)
    y_sample = _from_time_major(xs, b_s, t_s, nb_s)

    return (y_prompt, y_sample, jnp.stack(pool_p), jnp.stack(conv_p), jnp.stack(short_p), mem_k_p, mem_v_p,
            jnp.stack(pool_s), jnp.stack(conv_s), jnp.stack(short_s))
```

```python
import functools

import jax
import jax.numpy as jnp
from jax import lax
from jax.experimental import pallas as pl
from jax.experimental.pallas import tpu as pltpu

F32 = jnp.float32
BF16 = jnp.bfloat16

EPS = 1e-6
POOL_WINDOWS = (2, 4, 8, 16)
POOL_HIST = max(POOL_WINDOWS) - 1
CONV_TAPS = 31
CONV_HIST = CONV_TAPS - 1
SHORT_TAPS = 3
SHORT_HIST = SHORT_TAPS - 1
N_HEADS = 4
PAST_LEN = 16384

LANES = 128
SUBLANES = 8
MXU_COLS = 256
VMEM_LIMIT_BYTES = 56 * 1024 * 1024

TILE_ROWS = 512
SAMPLE_TILE_ROWS = 256
ATTN_TILE_ROWS = 1024
NORM_ROWS = 32
CONV_ROWS = 128
CONV_TAP_GROUP = 8
XATTN_SEQS = 8


def _mm(a, b):
    return jnp.dot(a, b, preferred_element_type=F32)


def _silu(x):
    return x * jax.nn.sigmoid(x)


def _rmsnorm(x, g):
    ms = jnp.mean(x * x, axis=-1, keepdims=True)
    return x * lax.rsqrt(ms + EPS) * g


def _softmax(s):
    e = jnp.exp(s - jnp.max(s, axis=-1, keepdims=True))
    return e / jnp.sum(e, axis=-1, keepdims=True)


def _norm_rows(dst_ref, src_ref, g_ref, dtype):
    rows = src_ref.shape[0]
    for r0 in range(0, rows, NORM_ROWS):
        rs = slice(r0, r0 + NORM_ROWS)
        dst_ref[rs, :] = _rmsnorm(src_ref[rs, :], g_ref[...]).astype(dtype)


def _window_rows(hist_ref, new_ref, start, n_rows, cols):
    hist_rows = hist_ref.shape[0]
    n_hist = min(max(hist_rows - start, 0), n_rows)
    parts = []
    if n_hist > 0:
        parts.append(hist_ref[start:start + n_hist, cols])
    if n_hist < n_rows:
        s = start + n_hist - hist_rows
        parts.append(new_ref[s:s + n_rows - n_hist, cols])
    return parts[0] if len(parts) == 1 else jnp.concatenate(parts, axis=0)


def _last_rows(dst_ref, hist_ref, new_ref):
    n, tm = dst_ref.shape[0], new_ref.shape[0]
    if tm >= n:
        dst_ref[...] = new_ref[tm - n:tm, :]
    else:
        dst_ref[0:n - tm, :] = hist_ref[tm:n, :]
        dst_ref[n - tm:n, :] = new_ref[...]


def _exact_zero_of(x):
    bits = lax.bitcast_convert_type(x, jnp.uint32)
    sixteen = jnp.full(bits.shape, 16, jnp.uint32)
    bits = lax.shift_right_logical(lax.shift_right_logical(bits, sixteen), sixteen)
    return lax.bitcast_convert_type(bits, F32)


def _lane_chunks(width):
    return [slice(c0, c0 + LANES) for c0 in range(0, width, LANES)]


def _even_kernel(*refs, nb, tt, start_pos, carry, batch_major_in):
    tm = nb * tt
    it = iter(refs)
    x_ref, hp_ref, hc_ref, nmix_ref, win_ref, poolw_ref, pscale_ref = (next(it) for _ in range(7))
    convw_ref, convb_ref, lng_ref, lnb_ref, wout_ref = (next(it) for _ in range(5))
    o_ref, newp_ref, newc_ref = next(it), next(it), next(it)
    h_scr, pool_scr, vnew_scr, ag_scr, bg_scr, c_scr, ac_scr, wtile_scr = (next(it) for _ in range(8))
    hc_scr = next(it) if carry else None
    if batch_major_in:
        xc_scr, xt_scr = next(it), next(it)

    i = pl.program_id(0)
    n_steps = pl.num_programs(0)
    w_a = pscale_ref.shape[-1]
    w_b = convb_ref.shape[-1]
    hp_rows = POOL_HIST * nb
    pool_group = w_a // len(POOL_WINDOWS)

    if carry:
        @pl.when(i == 0)
        def _():
            pool_scr[0:hp_rows, :] = hp_ref[...]
            hc_scr[...] = hc_ref[...]
        hist_c = hc_scr
    else:
        pool_scr[0:hp_rows, :] = hp_ref[...]
        hist_c = hc_ref

    if batch_major_in:
        for b in range(nb):
            for j, cs in enumerate(_lane_chunks(x_ref.shape[-1])):
                xc_scr[j, pl.ds(b, tt, stride=nb), :] = x_ref[b, :, cs]
        for j, cs in enumerate(_lane_chunks(x_ref.shape[-1])):
            xt_scr[:, cs] = xc_scr[j]
        x_src = xt_scr
    else:
        x_src = x_ref

    _norm_rows(h_scr, x_src, nmix_ref, BF16)

    def column_piece(col0, c0):
        return _mm(h_scr[...], win_ref[:, col0 + c0:col0 + c0 + MXU_COLS])
    for c0 in range(0, w_b, MXU_COLS):
        vnew_scr[:, c0:c0 + MXU_COLS] = column_piece(2 * w_a, c0) * jax.nn.sigmoid(column_piece(2 * w_a + w_b, c0))

    def project(dst_ref, dst_rows, col0, c0):
        def run():
            dst_ref[dst_rows, c0:c0 + MXU_COLS] = column_piece(col0, c0)
        return run
    pieces = ([project(pool_scr, slice(hp_rows, hp_rows + tm), 0, c0) for c0 in range(0, w_a, MXU_COLS)]
              + [project(ag_scr, slice(None), w_a, c0) for c0 in range(0, w_a, MXU_COLS)]
              + [project(bg_scr, slice(None), 2 * w_a + 2 * w_b, c0) for c0 in range(0, w_b, MXU_COLS)])
    conv_chunks = _lane_chunks(w_b)
    for k in range(CONV_TAPS):
        wtile_scr[k * SUBLANES:(k + 1) * SUBLANES, :] = jnp.broadcast_to(convw_ref[k:k + 1, :], (SUBLANES, w_b))

    def conv_chunk(n, cs):
        order = None
        for r0 in range(0, tm, CONV_ROWS):
            bias = jnp.broadcast_to(convb_ref[:, cs], (SUBLANES, LANES))
            if order is not None:
                bias = bias + order
            acc = jnp.concatenate([bias] * (CONV_ROWS // SUBLANES), axis=0)
            for k0 in range(0, CONV_TAPS, CONV_TAP_GROUP):
                taps = range(k0, min(k0 + CONV_TAP_GROUP, CONV_TAPS))
                win = _window_rows(hist_c, vnew_scr, r0 + k0 * nb, (len(taps) - 1) * nb + CONV_ROWS, cs)
                for k in taps:
                    wk = wtile_scr[k * SUBLANES:(k + 1) * SUBLANES, cs]
                    acc = acc + (win[(k - k0) * nb:(k - k0) * nb + CONV_ROWS, :]
                                 * jnp.concatenate([wk] * (CONV_ROWS // SUBLANES), axis=0))
            c_scr[r0:r0 + CONV_ROWS, cs] = acc
            order = _exact_zero_of(acc[0:SUBLANES, :])
        for run in pieces[n * len(pieces) // len(conv_chunks):(n + 1) * len(pieces) // len(conv_chunks)]:
            run()

    for n, cs in enumerate(conv_chunks):
        pl.when(i >= 0)(functools.partial(conv_chunk, n, cs))

    row = lax.broadcasted_iota(jnp.int32, (tm, LANES), 0)
    step = row >> (nb.bit_length() - 1)
    pos1 = step + (start_pos + 1 + (i * tt if carry else 0))
    for g, w in enumerate(POOL_WINDOWS):
        inv_cnt = 1.0 / jnp.minimum(pos1, w).astype(F32)
        a_parts = []
        for c0 in range(g * pool_group, (g + 1) * pool_group, LANES):
            e = pool_scr[(POOL_HIST + 1 - w) * nb:hp_rows + tm, c0:c0 + LANES]
            s, sh = e, 1
            while sh < w:
                n = s.shape[0]
                s = s[sh * nb:, :] + s[:n - sh * nb, :]
                sh *= 2
            a_parts.append((s * inv_cnt - e[(w - 1) * nb:, :]).astype(BF16))
        cs = slice(g * pool_group, (g + 1) * pool_group)
        pa = _mm(jnp.concatenate(a_parts, axis=1), poolw_ref[g])
        ac_scr[:, cs] = (pa * pscale_ref[:, cs] * _silu(ag_scr[:, cs])).astype(BF16)

    o_ref[...] = x_src[...] + _mm(ac_scr[:, 0:w_a], wout_ref[0:w_a, :])

    for r0 in range(0, tm, NORM_ROWS):
        rs = slice(r0, r0 + NORM_ROWS)
        c = c_scr[rs, :]
        d = c - jnp.mean(c, axis=-1, keepdims=True)
        var = jnp.mean(d * d, axis=-1, keepdims=True)
        y = d * lax.rsqrt(var + EPS) * lng_ref[...] + lnb_ref[...]
        ac_scr[rs, w_a:w_a + w_b] = (_silu(y) * _silu(bg_scr[rs, :])).astype(BF16)

    o_ref[...] += _mm(ac_scr[:, w_a:w_a + w_b], wout_ref[w_a:w_a + w_b, :])

    if carry:
        _last_rows(hc_scr, hc_scr, vnew_scr)
        hp_new = pool_scr[tm:tm + hp_rows, :]
        pool_scr[0:hp_rows, :] = hp_new

        @pl.when(i == n_steps - 1)
        def _():
            newp_ref[...] = pool_scr[0:hp_rows, :]
            newc_ref[...] = hc_scr[...]
    else:
        newp_ref[...] = pool_scr[tm:tm + hp_rows, :]
        _last_rows(newc_ref, hc_ref, vnew_scr)


def _odd_kernel(*refs, nb, tt, carry):
    tm = nb * tt
    it = iter(refs)
    x_ref, hs_ref, nmix_ref, win_ref, shortw_ref, wout_ref = (next(it) for _ in range(6))
    o_ref, news_ref = next(it), next(it)
    h_scr, unew_scr, gate_scr, y_scr = (next(it) for _ in range(4))
    hs_scr = next(it) if carry else None

    i = pl.program_id(0)
    n_steps = pl.num_programs(0)
    w_c = shortw_ref.shape[-1]
    d_model = x_ref.shape[-1]

    if carry:
        @pl.when(i == 0)
        def _():
            hs_scr[...] = hs_ref[...]
        hist = hs_scr
    else:
        hist = hs_ref

    _norm_rows(h_scr, x_ref, nmix_ref, BF16)

    for c0 in range(0, w_c, d_model):
        cs = slice(c0, c0 + d_model)
        gc = _mm(h_scr[...], win_ref[:, w_c + c0:w_c + c0 + d_model])
        xv = _mm(h_scr[...], win_ref[:, 2 * w_c + c0:2 * w_c + c0 + d_model])
        unew_scr[:, cs] = gc * xv
    for c0 in range(0, w_c, d_model):
        cs = slice(c0, c0 + d_model)
        gb = _mm(h_scr[...], win_ref[:, c0:c0 + d_model])
        gg = _mm(h_scr[...], win_ref[:, 3 * w_c + c0:3 * w_c + c0 + d_model])
        gate_scr[:, cs] = gb * _silu(gg)
    for cs in _lane_chunks(w_c):
        acc = _window_rows(hist, unew_scr, 0, tm, cs) * shortw_ref[0:1, cs]
        for k in range(1, SHORT_TAPS):
            acc = acc + _window_rows(hist, unew_scr, k * nb, tm, cs) * shortw_ref[k:k + 1, cs]
        y_scr[:, cs] = (gate_scr[:, cs] * acc).astype(BF16)

    o_ref[...] = x_ref[...] + _mm(y_scr[...], wout_ref[...])

    if carry:
        _last_rows(hs_scr, hs_scr, unew_scr)

        @pl.when(i == n_steps - 1)
        def _():
            news_ref[...] = hs_scr[...]
    else:
        _last_rows(news_ref, hs_ref, unew_scr)


def _project_q(h_scr, wq_ref, q_scr, d_head):
    chunks = d_head // LANES
    for hd in range(N_HEADS):
        q = _mm(h_scr[...], wq_ref[:, hd * d_head:(hd + 1) * d_head]) * d_head ** -0.5
        for c in range(chunks):
            q_scr[hd * chunks + c] = q[:, c * LANES:(c + 1) * LANES]


def _pair_slot(seq, j, chunks):
    return seq * N_HEADS + j // chunks, slice((j % chunks) * LANES, (j % chunks + 1) * LANES)


def _xattn_prompt_kernel(*refs, nb, tt, final, batch_major_out):
    it = iter(refs)
    x_ref, nx_ref, wq_ref, kt_ref, v_ref, wo_ref = (next(it) for _ in range(6))
    fn_ref = next(it) if final else None
    o_ref = next(it)
    h_scr, q_scr, oh_scr, qg_scr, y_scr = (next(it) for _ in range(5))
    d_head = v_ref.shape[-1]
    chunks = d_head // LANES
    n_chunks = N_HEADS * chunks

    _norm_rows(h_scr, x_ref, nx_ref, BF16)
    _project_q(h_scr, wq_ref, q_scr, d_head)
    for b in range(nb):
        for j in range(n_chunks):
            pair, cs = _pair_slot(b, j, chunks)
            qg_scr[pair, :, cs] = q_scr[j, pl.ds(b, tt, stride=nb), :].astype(BF16)
    s = jnp.einsum("pqd,pdk->pqk", qg_scr[...], kt_ref[...], preferred_element_type=F32)
    o = jnp.einsum("pqk,pkd->pqd", _softmax(s).astype(BF16), v_ref[...], preferred_element_type=F32)
    for b in range(nb):
        for j in range(n_chunks):
            pair, cs = _pair_slot(b, j, chunks)
            oh_scr[j, pl.ds(b, tt, stride=nb), :] = o[pair, :, cs]
    o_all = jnp.concatenate([oh_scr[j] for j in range(n_chunks)], axis=1)
    out = x_ref[...] + _mm(o_all.astype(BF16), wo_ref[...])

    if not (final or batch_major_out):
        o_ref[...] = out
        return
    y_scr[...] = out
    if final:
        _norm_rows(y_scr, y_scr, fn_ref, F32)
    if not batch_major_out:
        o_ref[...] = y_scr[...]
        return
    for j, cs in enumerate(_lane_chunks(y_scr.shape[-1])):
        q_scr[j] = y_scr[:, cs]
    for b in range(nb):
        for j, cs in enumerate(_lane_chunks(y_scr.shape[-1])):
            o_ref[b, :, cs] = q_scr[j, pl.ds(b, tt, stride=nb), :]


def _xattn_sample_kernel(*refs, nbs, final):
    it = iter(refs)
    x_ref, nx_ref, wq_ref, k_ref, v_ref, wo_ref = (next(it) for _ in range(6))
    fn_ref = next(it) if final else None
    o_ref = next(it)
    h_scr, q_scr, oh_scr, y_scr, qg_scr, kg_scr, vg_scr = (next(it) for _ in range(7))

    c = pl.program_id(1)
    tt = x_ref.shape[0] // nbs
    seqs = k_ref.shape[0]
    d_head = wq_ref.shape[-1] // N_HEADS
    chunks = d_head // LANES
    n_chunks = N_HEADS * chunks
    n_mem = k_ref.shape[1] // n_chunks

    @pl.when(c == 0)
    def _():
        _norm_rows(h_scr, x_ref, nx_ref, BF16)
        _project_q(h_scr, wq_ref, q_scr, d_head)

    for bl in range(seqs):
        rows = pl.ds(c * seqs + bl, tt, stride=nbs)
        for j in range(n_chunks):
            pair, cs = _pair_slot(bl, j, chunks)
            cache_rows = pl.ds((j % chunks) * N_HEADS + j // chunks, n_mem, stride=n_chunks)
            qg_scr[pair, :, cs] = q_scr[j, rows, :]
            kg_scr[pair, :, cs] = k_ref[bl, cache_rows, :].astype(BF16)
            vg_scr[pair, :, cs] = v_ref[bl, cache_rows, :].astype(BF16)
    s = jnp.einsum("pqd,pkd->pqk", qg_scr[...].astype(BF16), kg_scr[...], preferred_element_type=F32)
    o = jnp.einsum("pqk,pkd->pqd", _softmax(s).astype(BF16), vg_scr[...], preferred_element_type=F32)
    for bl in range(seqs):
        rows = pl.ds(c * seqs + bl, tt, stride=nbs)
        for j in range(n_chunks):
            pair, cs = _pair_slot(bl, j, chunks)
            oh_scr[j, rows, :] = o[pair, :, cs]

    @pl.when(c == pl.num_programs(1) - 1)
    def _():
        o_all = jnp.concatenate([oh_scr[j] for j in range(n_chunks)], axis=1)
        out = x_ref[...] + _mm(o_all.astype(BF16), wo_ref[...])
        if final:
            y_scr[...] = out
            _norm_rows(o_ref, y_scr, fn_ref, F32)
        else:
            o_ref[...] = out


def _memory_kv_kernel(*refs, n_casts):
    mem_ref, nmem_ref, wk_ref, wv_ref = refs[:4]
    cast_in = refs[4:4 + n_casts]
    kr_ref, vr_ref, kt_ref, vb_ref = refs[4 + n_casts:8 + n_casts]
    cast_out = refs[8 + n_casts:8 + 2 * n_casts]
    h_scr, wkb_scr, wvb_scr = refs[8 + 2 * n_casts:]
    for src, dst in zip(cast_in, cast_out):
        dst[...] = src[...].astype(BF16)

    @pl.when(pl.program_id(1) == 0)
    def _():
        wkb_scr[...] = wk_ref[...].astype(BF16)
        wvb_scr[...] = wv_ref[...].astype(BF16)

    n_mem = mem_ref.shape[0]
    d_head = vb_ref.shape[-1]
    chunks = d_head // LANES
    n_chunks = chunks * N_HEADS
    _norm_rows(h_scr, mem_ref, nmem_ref, BF16)
    for w_ref, r_ref in ((wkb_scr, kr_ref), (wvb_scr, vr_ref)):
        for hd in range(N_HEADS):
            kh = _mm(h_scr[...], w_ref[:, hd * d_head:(hd + 1) * d_head])
            if r_ref is kr_ref:
                kt_ref[hd] = kh.T.astype(BF16)
            else:
                vb_ref[hd] = kh.astype(BF16)
            for cc in range(chunks):
                r_ref[pl.ds(cc * N_HEADS + hd, n_mem, stride=n_chunks), :] = kh[:, cc * LANES:(cc + 1) * LANES]


def _const_spec(shape, layer=None):
    if layer is None:
        return pl.BlockSpec(shape, lambda *_: (0,) * len(shape), pipeline_mode=pl.Buffered(1))
    return pl.BlockSpec((None,) + shape, lambda *_: (layer,) + (0,) * len(shape), pipeline_mode=pl.Buffered(1))


def _row_spec(rows, cols):
    return pl.BlockSpec((rows, cols), lambda i: (i, 0))


def _params(n_axes):
    return pltpu.CompilerParams(dimension_semantics=("arbitrary",) * n_axes, vmem_limit_bytes=VMEM_LIMIT_BYTES)


def _chunk_major(tm, d):
    return pltpu.VMEM((d // LANES, tm, LANES), F32)


def _even_layer(x, hp, hc, p, e, l, *, nb, tt, start_pos, carry):
    batch_major_in = x.ndim == 3
    d = x.shape[-1]
    tm = nb * tt
    rows = x.shape[0] * x.shape[1] if batch_major_in else x.shape[0]
    w_a, w_b = p["pool_scale"].shape[-1], p["conv_b"].shape[-1]
    hp_rows, hc_rows = POOL_HIST * nb, CONV_HIST * nb
    assert not carry or tm >= hc_rows
    assert not batch_major_in or (carry and x.shape[0] == nb)
    n_groups = len(POOL_WINDOWS)

    args = [x, hp, hc, p["norm_mix"], p["w_in_even"], p["pool_w"], p["pool_scale"], p["conv_w"], p["conv_b"],
            p["ln_g"], p["ln_b"], p["w_out_even"]]
    hist_spec = (lambda r, c: _const_spec((r, c))) if carry else _row_spec
    x_spec = pl.BlockSpec((nb, tt, d), lambda i: (0, i, 0)) if batch_major_in else _row_spec(tm, d)
    specs = [x_spec, hist_spec(hp_rows, w_a), hist_spec(hc_rows, w_b),
             _const_spec((1, d), l), _const_spec((d, 2 * w_a + 3 * w_b), e),
             _const_spec((n_groups, w_a // n_groups, w_a // n_groups), e), _const_spec((1, w_a), e),
             _const_spec((CONV_TAPS, w_b), e), _const_spec((1, w_b), e), _const_spec((1, w_b), e),
             _const_spec((1, w_b), e), _const_spec((w_a + w_b, d), e)]

    state_spec = (lambda r, c: pl.BlockSpec((r, c), lambda i: (0, 0))) if carry else _row_spec
    n_state = 1 if carry else rows // tm
    scratch = [pltpu.VMEM((tm, d), BF16), pltpu.VMEM((hp_rows + tm, w_a), F32), pltpu.VMEM((tm, w_b), F32),
               pltpu.VMEM((tm, w_a), F32), pltpu.VMEM((tm, w_b), F32), pltpu.VMEM((tm, w_b), F32),
               pltpu.VMEM((tm, w_a + w_b), BF16), pltpu.VMEM((CONV_TAPS * SUBLANES, w_b), F32)]
    if carry:
        scratch.append(pltpu.VMEM((hc_rows, w_b), F32))
    if batch_major_in:
        scratch += [_chunk_major(tm, d), pltpu.VMEM((tm, d), F32)]
    return pl.pallas_call(
        functools.partial(_even_kernel, nb=nb, tt=tt, start_pos=start_pos, carry=carry,
                          batch_major_in=batch_major_in),
        grid=(rows // tm,),
        in_specs=specs,
        out_specs=[_row_spec(tm, d), state_spec(hp_rows, w_a), state_spec(hc_rows, w_b)],
        out_shape=[jax.ShapeDtypeStruct((rows, d), F32), jax.ShapeDtypeStruct((n_state * hp_rows, w_a), F32),
                   jax.ShapeDtypeStruct((n_state * hc_rows, w_b), F32)],
        scratch_shapes=scratch,
        compiler_params=_params(1),
        name=f"even_mixer_{l}_{'carry' if carry else 'group'}",
    )(*args)


def _odd_layer(x, hs, p, o, l, *, nb, tt, carry):
    rows, d = x.shape
    tm = nb * tt
    w_c = p["short_w"].shape[-1]
    hs_rows = SHORT_HIST * nb
    assert not carry or tm >= hs_rows

    args = [x, hs, p["norm_mix"], p["w_in_odd"], p["short_w"], p["w_out_odd"]]
    hist_spec = (lambda r, c: _const_spec((r, c))) if carry else _row_spec
    specs = [_row_spec(tm, d), hist_spec(hs_rows, w_c), _const_spec((1, d), l), _const_spec((d, 4 * w_c), o),
             _const_spec((SHORT_TAPS, w_c), o), _const_spec((w_c, d), o)]
    state_spec = (lambda r, c: pl.BlockSpec((r, c), lambda i: (0, 0))) if carry else _row_spec
    n_state = 1 if carry else rows // tm
    scratch = [pltpu.VMEM((tm, d), BF16), pltpu.VMEM((tm, w_c), F32), pltpu.VMEM((tm, w_c), F32),
               pltpu.VMEM((tm, w_c), BF16)]
    if carry:
        scratch.append(pltpu.VMEM((hs_rows, w_c), F32))
    return pl.pallas_call(
        functools.partial(_odd_kernel, nb=nb, tt=tt, carry=carry),
        grid=(rows // tm,),
        in_specs=specs,
        out_specs=[_row_spec(tm, d), state_spec(hs_rows, w_c)],
        out_shape=[jax.ShapeDtypeStruct((rows, d), F32), jax.ShapeDtypeStruct((n_state * hs_rows, w_c), F32)],
        scratch_shapes=scratch,
        compiler_params=_params(1),
        name=f"odd_mixer_{l}_{'carry' if carry else 'group'}",
    )(*args)


def _xattn_prompt(x, kt, vb, p, l, *, nb, tt, final_norm=None, batch_major_out=False):
    rows, d = x.shape
    tm = nb * tt
    n_pairs, d_head = nb * N_HEADS, d // N_HEADS
    assert kt.shape[1] == n_pairs and vb.shape[1] == n_pairs
    final = final_norm is not None
    args = [x, p["norm_x"], p["w_q"], kt, vb, p["w_o"]]
    specs = [_row_spec(tm, d), _const_spec((1, d), l), _const_spec((d, d), l), _const_spec(kt.shape[1:], l),
             _const_spec(vb.shape[1:], l), _const_spec((d, d), l)]
    if final:
        args, specs = args + [final_norm], specs + [_const_spec((1, d))]
    if batch_major_out:
        out_spec = pl.BlockSpec((nb, tt, d), lambda i: (0, i, 0))
        out_shape = jax.ShapeDtypeStruct((nb, rows // nb, d), F32)
    else:
        out_spec, out_shape = _row_spec(tm, d), jax.ShapeDtypeStruct((rows, d), F32)
    return pl.pallas_call(
        functools.partial(_xattn_prompt_kernel, nb=nb, tt=tt, final=final, batch_major_out=batch_major_out),
        grid=(rows // tm,),
        in_specs=specs,
        out_specs=out_spec,
        out_shape=out_shape,
        scratch_shapes=[pltpu.VMEM((tm, d), BF16), _chunk_major(tm, d), _chunk_major(tm, d),
                        pltpu.VMEM((n_pairs, tt, d_head), BF16), pltpu.VMEM((tm, d), F32)],
        compiler_params=_params(1),
        name=f"xattn_prompt_{l}",
    )(*args)


def _xattn_sample(x, kr, vr, p, l, *, nbs, tm, final_norm=None):
    rows, d = x.shape
    n_chunks = nbs // XATTN_SEQS
    final = final_norm is not None
    d_head = d // N_HEADS
    n_pairs = XATTN_SEQS * N_HEADS
    n_mem = kr.shape[2] * LANES // d
    cache_block = (XATTN_SEQS,) + kr.shape[2:]
    cache_spec = pl.BlockSpec((None,) + cache_block, lambda g, c: (l, g * n_chunks + c, 0, 0))
    args = [x, p["norm_x"], p["w_q"], kr, vr, p["w_o"]]
    specs = [pl.BlockSpec((tm, d), lambda g, c: (g, 0)), _const_spec((1, d), l), _const_spec((d, d), l),
             cache_spec, cache_spec, _const_spec((d, d), l)]
    if final:
        args, specs = args + [final_norm], specs + [_const_spec((1, d))]
    return pl.pallas_call(
        functools.partial(_xattn_sample_kernel, nbs=nbs, final=final),
        grid=(rows // tm, n_chunks),
        in_specs=specs,
        out_specs=pl.BlockSpec((tm, d), lambda g, c: (g, 0)),
        out_shape=jax.ShapeDtypeStruct((rows, d), F32),
        scratch_shapes=[pltpu.VMEM((tm, d), BF16), _chunk_major(tm, d), _chunk_major(tm, d), pltpu.VMEM((tm, d), F32),
                        pltpu.VMEM((n_pairs, tm // nbs, d_head), F32), pltpu.VMEM((n_pairs, n_mem, d_head), BF16),
                        pltpu.VMEM((n_pairs, n_mem, d_head), BF16)],
        compiler_params=_params(2),
        name=f"xattn_sample_{l}",
    )(*args)


def _memory_kv(mem, p, w_k, w_v, casts, n_seq, n_mem, d_head):
    d = mem.shape[-1]
    depth = w_k.shape[0]
    n_steps = depth * n_seq
    cast_specs = []
    for w in casts:
        assert w.shape[0] % (n_steps * 2 * SUBLANES) == 0
        cast_specs.append(pl.BlockSpec((w.shape[0] // n_steps, w.shape[1]), lambda l, b: (l * n_seq + b, 0)))
    r_rows = n_mem * d // LANES
    r_spec = pl.BlockSpec((None, None, r_rows, LANES), lambda l, b: (l, b, 0, 0))
    kt_spec = pl.BlockSpec((None, None, N_HEADS, d_head, n_mem), lambda l, b: (l, b, 0, 0, 0))
    vb_spec = pl.BlockSpec((None, None, N_HEADS, n_mem, d_head), lambda l, b: (l, b, 0, 0, 0))
    w_spec = pl.BlockSpec((None, d, d), lambda l, b: (l, 0, 0))
    r_shape = jax.ShapeDtypeStruct((depth, n_seq, r_rows, LANES), F32)
    return pl.pallas_call(
        functools.partial(_memory_kv_kernel, n_casts=len(casts)),
        grid=(depth, n_seq),
        in_specs=[pl.BlockSpec((n_mem, d), lambda l, b: (b, 0)), pl.BlockSpec((None, 1, d), lambda l, b: (l, 0, 0)),
                  w_spec, w_spec] + cast_specs,
        out_specs=[r_spec, r_spec, kt_spec, vb_spec] + cast_specs,
        out_shape=[r_shape, r_shape, jax.ShapeDtypeStruct((depth, n_seq, N_HEADS, d_head, n_mem), BF16),
                   jax.ShapeDtypeStruct((depth, n_seq, N_HEADS, n_mem, d_head), BF16)]
        + [jax.ShapeDtypeStruct(w.shape, BF16) for w in casts],
        scratch_shapes=[pltpu.VMEM((n_mem, d), BF16), pltpu.VMEM((d, d), BF16), pltpu.VMEM((d, d), BF16)],
        compiler_params=_params(2),
        name="memory_kv",
    )(mem, p["norm_mem"], w_k, w_v, *casts)


def _to_time_major(x, nb):
    b, t, c = x.shape
    return x.reshape(b // nb, nb, t, c).transpose(0, 2, 1, 3).reshape(b * t, c)


def _from_time_major(y, b, t, nb):
    c = y.shape[-1]
    return y.reshape(b // nb, t, nb, c).transpose(0, 2, 1, 3).reshape(b, t, c)


def _cache_rows(c):
    *lead, n_mem, heads, d_head = c.shape
    chunks = d_head // LANES
    r = c.reshape(*lead, n_mem, heads, chunks, LANES)
    return jnp.swapaxes(r, -3, -2).reshape(*lead, n_mem * chunks * heads, LANES)


def _cache_from_rows(r, n_mem, heads, d_head):
    *lead, _, _ = r.shape
    chunks = d_head // LANES
    c = r.reshape(*lead, n_mem, chunks, heads, LANES)
    return jnp.swapaxes(c, -3, -2).reshape(*lead, n_mem, heads, d_head)


def kernel(x_prompt, x_sample, state_pool, state_conv, state_short, cache_mem_k, cache_mem_v, mem_prompt, norm_mix, w_in_even, pool_w, pool_scale, conv_w, conv_b, ln_g, ln_b, w_out_even, w_in_odd, short_w, w_out_odd, norm_x, norm_mem, w_q, w_k, w_v, w_o, final_norm):
    b_p, t_p, d = x_prompt.shape
    b_s, t_s, _ = x_sample.shape
    depth = norm_mix.shape[0]
    n_mem = mem_prompt.shape[1]
    heads, d_head = cache_mem_k.shape[-2:]
    assert heads == N_HEADS and d_head % LANES == 0 and depth >= 2
    assert d_head & (d_head - 1) == 0 and d_head.bit_length() % 2 == 1, "the score scale must be a power of two"

    nb_p, tt_p, tt_a = b_p, TILE_ROWS // b_p, ATTN_TILE_ROWS // b_p
    nb_s, tt_s = SAMPLE_TILE_ROWS // t_s, t_s
    for nb in (nb_p, nb_s):
        assert nb % SUBLANES == 0 and nb & (nb - 1) == 0
    assert t_p % tt_p == 0 and t_p % tt_a == 0 and b_s % nb_s == 0 and nb_s % XATTN_SEQS == 0

    row = lambda a: a.reshape(a.shape[0], 1, a.shape[-1])
    p = {
        "norm_mix": row(norm_mix), "norm_x": row(norm_x), "norm_mem": row(norm_mem),
        "pool_scale": row(pool_scale), "conv_w": conv_w, "conv_b": row(conv_b), "ln_g": row(ln_g), "ln_b": row(ln_b),
        "short_w": short_w,
    }
    fnorm = final_norm.reshape(1, d)
    w_a, w_b, w_c = pool_scale.shape[-1], conv_b.shape[-1], short_w.shape[-1]

    trunk = {"w_in_even": w_in_even, "pool_w": pool_w, "w_out_even": w_out_even, "w_in_odd": w_in_odd,
             "w_out_odd": w_out_odd, "w_q": w_q, "w_o": w_o}
    kr, vr, kt, vb, *cast = _memory_kv(mem_prompt.reshape(b_p * n_mem, d), p, w_k, w_v,
                                       [w.reshape(-1, w.shape[-1]) for w in trunk.values()], b_p, n_mem, d_head)
    p.update({name: c.reshape(w.shape) for (name, w), c in zip(trunk.items(), cast)})
    mem_k_p = _cache_from_rows(kr, n_mem, heads, d_head)
    mem_v_p = _cache_from_rows(vr, n_mem, heads, d_head)
    kt = kt.reshape(depth, b_p * heads, d_head, n_mem)
    vb = vb.reshape(depth, b_p * heads, n_mem, d_head)

    xp = x_prompt
    zp = jnp.zeros((POOL_HIST * nb_p, w_a), F32)
    zc = jnp.zeros((CONV_HIST * nb_p, w_b), F32)
    zs = jnp.zeros((SHORT_HIST * nb_p, w_c), F32)
    pool_p, conv_p, short_p = [], [], []
    for l in range(depth):
        last = l == depth - 1
        if l % 2 == 0:
            xp, sp, sc = _even_layer(xp, zp, zc, p, l // 2, l, nb=nb_p, tt=tt_p, start_pos=0, carry=True)
            pool_p.append(_from_time_major(sp, b_p, POOL_HIST, nb_p))
            conv_p.append(_from_time_major(sc, b_p, CONV_HIST, nb_p))
        else:
            xp, ss = _odd_layer(xp, zs, p, l // 2, l, nb=nb_p, tt=tt_p, carry=True)
            short_p.append(_from_time_major(ss, b_p, SHORT_HIST, nb_p))
        xp = _xattn_prompt(xp, kt, vb, p, l, nb=nb_p, tt=tt_a, final_norm=fnorm if last else None,
                           batch_major_out=last)
    y_prompt = xp

    krs, vrs = _cache_rows(cache_mem_k), _cache_rows(cache_mem_v)
    xs = _to_time_major(x_sample, nb_s)
    pool_s, conv_s, short_s = [], [], []
    for l in range(depth):
        if l % 2 == 0:
            e = l // 2
            xs, sp, sc = _even_layer(xs, _to_time_major(state_pool[e], nb_s), _to_time_major(state_conv[e], nb_s),
                                     p, e, l, nb=nb_s, tt=tt_s, start_pos=PAST_LEN, carry=False)
            pool_s.append(_from_time_major(sp, b_s, POOL_HIST, nb_s))
            conv_s.append(_from_time_major(sc, b_s, CONV_HIST, nb_s))
        else:
            o = l // 2
            xs, ss = _odd_layer(xs, _to_time_major(state_short[o], nb_s), p, o, l, nb=nb_s, tt=tt_s, carry=False)
            short_s.append(_from_time_major(ss, b_s, SHORT_HIST, nb_s))
        xs = _xattn_sample(xs, krs, vrs, p, l, nbs=nb_s, tm=nb_s * tt_s,
                           final_norm=fnorm if l == depth - 1 else None)
    y_sample = _from_time_major(xs, b_s, t_s, nb_s)

    return (y_prompt, y_sample, jnp.stack(pool_p), jnp.stack(conv_p), jnp.stack(short_p), mem_k_p, mem_v_p,
            jnp.stack(pool_s), jnp.stack(conv_s), jnp.stack(short_s))
```

```python
import functools
import math

import jax
import jax.numpy as jnp
from jax import lax
from jax.experimental import pallas as pl
from jax.experimental.pallas import tpu as pltpu

F32 = jnp.float32
BF16 = jnp.bfloat16

EPS = 1e-6
POOL_WINDOWS = (2, 4, 8, 16)
POOL_HIST = max(POOL_WINDOWS) - 1
CONV_TAPS = 31
CONV_HIST = CONV_TAPS - 1
SHORT_TAPS = 3
SHORT_HIST = SHORT_TAPS - 1
N_HEADS = 4
PAST_LEN = 16384

LANES = 128
SUBLANES = 8
MXU_COLS = 256
VMEM_BYTES = 64 * 1024 * 1024

TILE_ROWS = 512
SAMPLE_TILE_ROWS = 256
ATTN_TILE_ROWS = 1024
NORM_ROWS = 32
CONV_ROWS = 128
CONV_TAP_GROUP = 8
XATTN_SEQS = 8


def _mm(a, b):
    return jnp.dot(a, b, preferred_element_type=F32)


def _silu(x):
    return x * jax.nn.sigmoid(x)


def _rmsnorm(x, g):
    ms = jnp.mean(x * x, axis=-1, keepdims=True)
    return x * lax.rsqrt(ms + EPS) * g


def _softmax(s):
    e = jnp.exp(s - jnp.max(s, axis=-1, keepdims=True))
    return e / jnp.sum(e, axis=-1, keepdims=True)


def _norm_rows(dst_ref, src_ref, g_ref, dtype):
    rows = src_ref.shape[0]
    for r0 in range(0, rows, NORM_ROWS):
        rs = slice(r0, r0 + NORM_ROWS)
        dst_ref[rs, :] = _rmsnorm(src_ref[rs, :], g_ref[...]).astype(dtype)


def _window_rows(hist_ref, new_ref, start, n_rows, cols):
    hist_rows = hist_ref.shape[0]
    n_hist = min(max(hist_rows - start, 0), n_rows)
    parts = []
    if n_hist > 0:
        parts.append(hist_ref[start:start + n_hist, cols])
    if n_hist < n_rows:
        s = start + n_hist - hist_rows
        parts.append(new_ref[s:s + n_rows - n_hist, cols])
    return parts[0] if len(parts) == 1 else jnp.concatenate(parts, axis=0)


def _last_rows(dst_ref, hist_ref, new_ref):
    n, tm = dst_ref.shape[0], new_ref.shape[0]
    if tm >= n:
        dst_ref[...] = new_ref[tm - n:tm, :]
    else:
        dst_ref[0:n - tm, :] = hist_ref[tm:n, :]
        dst_ref[n - tm:n, :] = new_ref[...]


def _exact_zero_of(x):
    bits = lax.bitcast_convert_type(x, jnp.uint32)
    sixteen = jnp.full(bits.shape, 16, jnp.uint32)
    bits = lax.shift_right_logical(lax.shift_right_logical(bits, sixteen), sixteen)
    return lax.bitcast_convert_type(bits, F32)


def _lane_chunks(width):
    return [slice(c0, c0 + LANES) for c0 in range(0, width, LANES)]


def _even_kernel(*refs, nb, tt, start_pos, carry, batch_major_in):
    tm = nb * tt
    it = iter(refs)
    x_ref, hp_ref, hc_ref, nmix_ref, win_ref, poolw_ref, pscale_ref = (next(it) for _ in range(7))
    convw_ref, convb_ref, lng_ref, lnb_ref, wout_ref = (next(it) for _ in range(5))
    o_ref, newp_ref, newc_ref = next(it), next(it), next(it)
    h_scr, pool_scr, vnew_scr, ag_scr, bg_scr, c_scr, ac_scr, wtile_scr = (next(it) for _ in range(8))
    hc_scr = next(it) if carry else None
    if batch_major_in:
        xc_scr, xt_scr = next(it), next(it)

    i = pl.program_id(0)
    n_steps = pl.num_programs(0)
    w_a = pscale_ref.shape[-1]
    w_b = convb_ref.shape[-1]
    hp_rows = POOL_HIST * nb
    pool_group = w_a // len(POOL_WINDOWS)

    if carry:
        @pl.when(i == 0)
        def _():
            pool_scr[0:hp_rows, :] = hp_ref[...]
            hc_scr[...] = hc_ref[...]
        hist_c = hc_scr
    else:
        pool_scr[0:hp_rows, :] = hp_ref[...]
        hist_c = hc_ref

    if batch_major_in:
        for b in range(nb):
            for j, cs in enumerate(_lane_chunks(x_ref.shape[-1])):
                xc_scr[j, pl.ds(b, tt, stride=nb), :] = x_ref[b, :, cs]
        for j, cs in enumerate(_lane_chunks(x_ref.shape[-1])):
            xt_scr[:, cs] = xc_scr[j]
        x_src = xt_scr
    else:
        x_src = x_ref

    _norm_rows(h_scr, x_src, nmix_ref, BF16)

    def column_piece(col0, c0):
        return _mm(h_scr[...], win_ref[:, col0 + c0:col0 + c0 + MXU_COLS])
    for c0 in range(0, w_b, MXU_COLS):
        vnew_scr[:, c0:c0 + MXU_COLS] = column_piece(2 * w_a, c0) * jax.nn.sigmoid(column_piece(2 * w_a + w_b, c0))

    def project(dst_ref, dst_rows, col0, c0):
        def run():
            dst_ref[dst_rows, c0:c0 + MXU_COLS] = column_piece(col0, c0)
        return run
    pieces = ([project(pool_scr, slice(hp_rows, hp_rows + tm), 0, c0) for c0 in range(0, w_a, MXU_COLS)]
              + [project(ag_scr, slice(None), w_a, c0) for c0 in range(0, w_a, MXU_COLS)]
              + [project(bg_scr, slice(None), 2 * w_a + 2 * w_b, c0) for c0 in range(0, w_b, MXU_COLS)])
    conv_chunks = _lane_chunks(w_b)
    for k in range(CONV_TAPS):
        wtile_scr[k * SUBLANES:(k + 1) * SUBLANES, :] = jnp.broadcast_to(convw_ref[k:k + 1, :], (SUBLANES, w_b))

    def conv_chunk(n, cs):
        order = None
        for r0 in range(0, tm, CONV_ROWS):
            bias = jnp.broadcast_to(convb_ref[:, cs], (SUBLANES, LANES))
            if order is not None:
                bias = bias + order
            acc = jnp.concatenate([bias] * (CONV_ROWS // SUBLANES), axis=0)
            for k0 in range(0, CONV_TAPS, CONV_TAP_GROUP):
                taps = range(k0, min(k0 + CONV_TAP_GROUP, CONV_TAPS))
                win = _window_rows(hist_c, vnew_scr, r0 + k0 * nb, (len(taps) - 1) * nb + CONV_ROWS, cs)
                for k in taps:
                    wk = wtile_scr[k * SUBLANES:(k + 1) * SUBLANES, cs]
                    acc = acc + (win[(k - k0) * nb:(k - k0) * nb + CONV_ROWS, :]
                                 * jnp.concatenate([wk] * (CONV_ROWS // SUBLANES), axis=0))
            c_scr[r0:r0 + CONV_ROWS, cs] = acc
            order = _exact_zero_of(acc[0:SUBLANES, :])
        for run in pieces[n * len(pieces) // len(conv_chunks):(n + 1) * len(pieces) // len(conv_chunks)]:
            run()

    for n, cs in enumerate(conv_chunks):
        pl.when(i >= 0)(functools.partial(conv_chunk, n, cs))

    row = lax.broadcasted_iota(jnp.int32, (tm, LANES), 0)
    step = row >> (nb.bit_length() - 1)
    pos1 = step + (start_pos + 1 + (i * tt if carry else 0))
    for g, w in enumerate(POOL_WINDOWS):
        inv_cnt = 1.0 / jnp.minimum(pos1, w).astype(F32)
        a_parts = []
        for c0 in range(g * pool_group, (g + 1) * pool_group, LANES):
            e = pool_scr[(POOL_HIST + 1 - w) * nb:hp_rows + tm, c0:c0 + LANES]
            s, sh = e, 1
            while sh < w:
                n = s.shape[0]
                s = s[sh * nb:, :] + s[:n - sh * nb, :]
                sh *= 2
            a_parts.append((s * inv_cnt - e[(w - 1) * nb:, :]).astype(BF16))
        cs = slice(g * pool_group, (g + 1) * pool_group)
        pa = _mm(jnp.concatenate(a_parts, axis=1), poolw_ref[g])
        ac_scr[:, cs] = (pa * pscale_ref[:, cs] * _silu(ag_scr[:, cs])).astype(BF16)

    o_ref[...] = x_src[...] + _mm(ac_scr[:, 0:w_a], wout_ref[0:w_a, :])

    for r0 in range(0, tm, NORM_ROWS):
        rs = slice(r0, r0 + NORM_ROWS)
        c = c_scr[rs, :]
        d = c - jnp.mean(c, axis=-1, keepdims=True)
        var = jnp.mean(d * d, axis=-1, keepdims=True)
        y = d * lax.rsqrt(var + EPS) * lng_ref[...] + lnb_ref[...]
        ac_scr[rs, w_a:w_a + w_b] = (_silu(y) * _silu(bg_scr[rs, :])).astype(BF16)

    o_ref[...] += _mm(ac_scr[:, w_a:w_a + w_b], wout_ref[w_a:w_a + w_b, :])

    if carry:
        _last_rows(hc_scr, hc_scr, vnew_scr)
        hp_new = pool_scr[tm:tm + hp_rows, :]
        pool_scr[0:hp_rows, :] = hp_new

        @pl.when(i == n_steps - 1)
        def _():
            newp_ref[...] = pool_scr[0:hp_rows, :]
            newc_ref[...] = hc_scr[...]
    else:
        newp_ref[...] = pool_scr[tm:tm + hp_rows, :]
        _last_rows(newc_ref, hc_ref, vnew_scr)


def _odd_kernel(*refs, nb, tt, carry):
    tm = nb * tt
    it = iter(refs)
    x_ref, hs_ref, nmix_ref, win_ref, shortw_ref, wout_ref = (next(it) for _ in range(6))
    o_ref, news_ref = next(it), next(it)
    h_scr, unew_scr, y_scr = (next(it) for _ in range(3))
    hs_scr = next(it) if carry else None

    i = pl.program_id(0)
    n_steps = pl.num_programs(0)
    w_c = shortw_ref.shape[-1]
    d_model = x_ref.shape[-1]

    if carry:
        @pl.when(i == 0)
        def _():
            hs_scr[...] = hs_ref[...]
        hist = hs_scr
    else:
        hist = hs_ref

    _norm_rows(h_scr, x_ref, nmix_ref, BF16)

    def column_piece(col0, c0):
        return _mm(h_scr[...], win_ref[:, col0 + c0:col0 + c0 + MXU_COLS])
    for c0 in range(0, w_c, MXU_COLS):
        unew_scr[:, c0:c0 + MXU_COLS] = column_piece(w_c, c0) * column_piece(2 * w_c, c0)
    for c0 in range(0, w_c, MXU_COLS):
        gate = column_piece(0, c0) * _silu(column_piece(3 * w_c, c0))
        for c1 in range(0, MXU_COLS, LANES):
            cs = slice(c0 + c1, c0 + c1 + LANES)
            acc = _window_rows(hist, unew_scr, 0, tm, cs) * shortw_ref[0:1, cs]
            for k in range(1, SHORT_TAPS):
                acc = acc + _window_rows(hist, unew_scr, k * nb, tm, cs) * shortw_ref[k:k + 1, cs]
            y_scr[:, cs] = (gate[:, c1:c1 + LANES] * acc).astype(BF16)

    o_ref[...] = x_ref[...] + _mm(y_scr[...], wout_ref[...])

    if carry:
        _last_rows(hs_scr, hs_scr, unew_scr)

        @pl.when(i == n_steps - 1)
        def _():
            news_ref[...] = hs_scr[...]
    else:
        _last_rows(news_ref, hs_ref, unew_scr)


def _project_q(h_scr, wq_ref, q_scr, d_head):
    chunks = d_head // LANES
    for hd in range(N_HEADS):
        q = _mm(h_scr[...], wq_ref[:, hd * d_head:(hd + 1) * d_head]) * d_head ** -0.5
        for c in range(chunks):
            q_scr[hd * chunks + c] = q[:, c * LANES:(c + 1) * LANES]


def _pair_slot(seq, j, chunks):
    return seq * N_HEADS + j // chunks, slice((j % chunks) * LANES, (j % chunks + 1) * LANES)


def _xattn_prompt_kernel(*refs, nb, tt, final, batch_major_out):
    it = iter(refs)
    x_ref, nx_ref, wq_ref, kt_ref, v_ref, wo_ref = (next(it) for _ in range(6))
    fn_ref = next(it) if final else None
    o_ref = next(it)
    h_scr, q_scr, oh_scr, qg_scr, y_scr = (next(it) for _ in range(5))
    d_head = v_ref.shape[-1]
    chunks = d_head // LANES
    n_chunks = N_HEADS * chunks

    _norm_rows(h_scr, x_ref, nx_ref, BF16)
    _project_q(h_scr, wq_ref, q_scr, d_head)
    for b in range(nb):
        for j in range(n_chunks):
            pair, cs = _pair_slot(b, j, chunks)
            qg_scr[pair, :, cs] = q_scr[j, pl.ds(b, tt, stride=nb), :].astype(BF16)
    s = jnp.einsum("pqd,pdk->pqk", qg_scr[...], kt_ref[...], preferred_element_type=F32)
    o = jnp.einsum("pqk,pkd->pqd", _softmax(s).astype(BF16), v_ref[...], preferred_element_type=F32)
    for b in range(nb):
        for j in range(n_chunks):
            pair, cs = _pair_slot(b, j, chunks)
            oh_scr[j, pl.ds(b, tt, stride=nb), :] = o[pair, :, cs]
    o_all = jnp.concatenate([oh_scr[j] for j in range(n_chunks)], axis=1)
    out = x_ref[...] + _mm(o_all.astype(BF16), wo_ref[...])

    if not (final or batch_major_out):
        o_ref[...] = out
        return
    y_scr[...] = out
    if final:
        _norm_rows(y_scr, y_scr, fn_ref, F32)
    if not batch_major_out:
        o_ref[...] = y_scr[...]
        return
    for j, cs in enumerate(_lane_chunks(y_scr.shape[-1])):
        q_scr[j] = y_scr[:, cs]
    for b in range(nb):
        for j, cs in enumerate(_lane_chunks(y_scr.shape[-1])):
            o_ref[b, :, cs] = q_scr[j, pl.ds(b, tt, stride=nb), :]


def _xattn_sample_kernel(*refs, nbs, final):
    it = iter(refs)
    x_ref, nx_ref, wq_ref, k_ref, v_ref, wo_ref = (next(it) for _ in range(6))
    fn_ref = next(it) if final else None
    o_ref = next(it)
    h_scr, q_scr, oh_scr, y_scr, qg_scr, kg_scr, vg_scr = (next(it) for _ in range(7))

    c = pl.program_id(1)
    tt = x_ref.shape[0] // nbs
    seqs = k_ref.shape[0]
    d_head = wq_ref.shape[-1] // N_HEADS
    chunks = d_head // LANES
    n_chunks = N_HEADS * chunks
    n_mem = k_ref.shape[1] // n_chunks

    @pl.when(c == 0)
    def _():
        _norm_rows(h_scr, x_ref, nx_ref, BF16)
        _project_q(h_scr, wq_ref, q_scr, d_head)

    for bl in range(seqs):
        rows = pl.ds(c * seqs + bl, tt, stride=nbs)
        for j in range(n_chunks):
            pair, cs = _pair_slot(bl, j, chunks)
            cache_rows = pl.ds((j % chunks) * N_HEADS + j // chunks, n_mem, stride=n_chunks)
            qg_scr[pair, :, cs] = q_scr[j, rows, :]
            kg_scr[pair, :, cs] = k_ref[bl, cache_rows, :].astype(BF16)
            vg_scr[pair, :, cs] = v_ref[bl, cache_rows, :].astype(BF16)
    s = jnp.einsum("pqd,pkd->pqk", qg_scr[...].astype(BF16), kg_scr[...], preferred_element_type=F32)
    o = jnp.einsum("pqk,pkd->pqd", _softmax(s).astype(BF16), vg_scr[...], preferred_element_type=F32)
    for bl in range(seqs):
        rows = pl.ds(c * seqs + bl, tt, stride=nbs)
        for j in range(n_chunks):
            pair, cs = _pair_slot(bl, j, chunks)
            oh_scr[j, rows, :] = o[pair, :, cs]

    @pl.when(c == pl.num_programs(1) - 1)
    def _():
        o_all = jnp.concatenate([oh_scr[j] for j in range(n_chunks)], axis=1)
        out = x_ref[...] + _mm(o_all.astype(BF16), wo_ref[...])
        if final:
            y_scr[...] = out
            _norm_rows(o_ref, y_scr, fn_ref, F32)
        else:
            o_ref[...] = out


def _memory_kv_kernel(*refs, n_casts):
    mem_ref, nmem_ref, wk_ref, wv_ref = refs[:4]
    cast_in = refs[4:4 + n_casts]
    kr_ref, vr_ref, kt_ref, vb_ref = refs[4 + n_casts:8 + n_casts]
    cast_out = refs[8 + n_casts:8 + 2 * n_casts]
    h_scr, wkb_scr, wvb_scr = refs[8 + 2 * n_casts:]
    for src, dst in zip(cast_in, cast_out):
        dst[...] = src[...].astype(BF16)

    @pl.when(pl.program_id(1) == 0)
    def _():
        wkb_scr[...] = wk_ref[...].astype(BF16)
        wvb_scr[...] = wv_ref[...].astype(BF16)

    n_mem = mem_ref.shape[0]
    d_head = vb_ref.shape[-1]
    chunks = d_head // LANES
    n_chunks = chunks * N_HEADS
    _norm_rows(h_scr, mem_ref, nmem_ref, BF16)
    for w_ref, r_ref in ((wkb_scr, kr_ref), (wvb_scr, vr_ref)):
        for hd in range(N_HEADS):
            kh = _mm(h_scr[...], w_ref[:, hd * d_head:(hd + 1) * d_head])
            if r_ref is kr_ref:
                kt_ref[hd] = kh.T.astype(BF16)
            else:
                vb_ref[hd] = kh.astype(BF16)
            for cc in range(chunks):
                r_ref[pl.ds(cc * N_HEADS + hd, n_mem, stride=n_chunks), :] = kh[:, cc * LANES:(cc + 1) * LANES]


def _const_spec(shape, layer=None):
    if layer is None:
        return pl.BlockSpec(shape, lambda *_: (0,) * len(shape), pipeline_mode=pl.Buffered(1))
    return pl.BlockSpec((None,) + shape, lambda *_: (layer,) + (0,) * len(shape), pipeline_mode=pl.Buffered(1))


def _row_spec(rows, cols):
    return pl.BlockSpec((rows, cols), lambda i: (i, 0))


def _buffer_bytes(shape, dtype):
    itemsize = jnp.dtype(dtype).itemsize
    dims = [s for s in shape if s is not None]
    rows = SUBLANES * 4 // itemsize
    if len(dims) >= 2:
        dims[-2] = -(-dims[-2] // rows) * rows
    dims[-1] = -(-dims[-1] // LANES) * LANES
    return math.prod(dims) * itemsize


def _pallas(body, *, grid, args, in_specs, out_specs, out_shape, scratch, value_bytes, name):
    single = not isinstance(out_specs, (list, tuple))
    o_specs, o_shapes = ([out_specs], [out_shape]) if single else (out_specs, out_shape)
    buffers = lambda spec: spec.pipeline_mode.buffer_count if spec.pipeline_mode is not None else 2
    limit = (sum(_buffer_bytes(s.block_shape, a.dtype) * buffers(s) for s, a in zip(in_specs, args))
             + sum(_buffer_bytes(s.block_shape, o.dtype) * buffers(s) for s, o in zip(o_specs, o_shapes))
             + sum(_buffer_bytes(m.shape, m.dtype) for m in scratch) + value_bytes)
    assert limit <= VMEM_BYTES, (name, limit)
    return pl.pallas_call(
        body, grid=grid, in_specs=in_specs, out_specs=out_specs, out_shape=out_shape, scratch_shapes=scratch,
        compiler_params=pltpu.CompilerParams(dimension_semantics=("arbitrary",) * len(grid), vmem_limit_bytes=limit),
        name=name)(*args)


def _chunk_major(tm, d):
    return pltpu.VMEM((d // LANES, tm, LANES), F32)


def _even_layer(x, hp, hc, p, e, l, *, nb, tt, start_pos, carry):
    batch_major_in = x.ndim == 3
    d = x.shape[-1]
    tm = nb * tt
    rows = x.shape[0] * x.shape[1] if batch_major_in else x.shape[0]
    w_a, w_b = p["pool_scale"].shape[-1], p["conv_b"].shape[-1]
    hp_rows, hc_rows = POOL_HIST * nb, CONV_HIST * nb
    assert not carry or tm >= hc_rows
    assert not batch_major_in or (carry and x.shape[0] == nb)
    n_groups = len(POOL_WINDOWS)

    args = [x, hp, hc, p["norm_mix"], p["w_in_even"], p["pool_w"], p["pool_scale"], p["conv_w"], p["conv_b"],
            p["ln_g"], p["ln_b"], p["w_out_even"]]
    hist_spec = (lambda r, c: _const_spec((r, c))) if carry else _row_spec
    x_spec = pl.BlockSpec((nb, tt, d), lambda i: (0, i, 0)) if batch_major_in else _row_spec(tm, d)
    specs = [x_spec, hist_spec(hp_rows, w_a), hist_spec(hc_rows, w_b),
             _const_spec((1, d), l), _const_spec((d, 2 * w_a + 3 * w_b), e),
             _const_spec((n_groups, w_a // n_groups, w_a // n_groups), e), _const_spec((1, w_a), e),
             _const_spec((CONV_TAPS, w_b), e), _const_spec((1, w_b), e), _const_spec((1, w_b), e),
             _const_spec((1, w_b), e), _const_spec((w_a + w_b, d), e)]

    state_spec = (lambda r, c: pl.BlockSpec((r, c), lambda i: (0, 0))) if carry else _row_spec
    n_state = 1 if carry else rows // tm
    scratch = [pltpu.VMEM((tm, d), BF16), pltpu.VMEM((hp_rows + tm, w_a), F32), pltpu.VMEM((tm, w_b), F32),
               pltpu.VMEM((tm, w_a), F32), pltpu.VMEM((tm, w_b), F32), pltpu.VMEM((tm, w_b), F32),
               pltpu.VMEM((tm, w_a + w_b), BF16), pltpu.VMEM((CONV_TAPS * SUBLANES, w_b), F32)]
    if carry:
        scratch.append(pltpu.VMEM((hc_rows, w_b), F32))
    if batch_major_in:
        scratch += [_chunk_major(tm, d), pltpu.VMEM((tm, d), F32)]
    return _pallas(
        functools.partial(_even_kernel, nb=nb, tt=tt, start_pos=start_pos, carry=carry,
                          batch_major_in=batch_major_in),
        grid=(rows // tm,), args=args, in_specs=specs,
        out_specs=[_row_spec(tm, d), state_spec(hp_rows, w_a), state_spec(hc_rows, w_b)],
        out_shape=[jax.ShapeDtypeStruct((rows, d), F32), jax.ShapeDtypeStruct((n_state * hp_rows, w_a), F32),
                   jax.ShapeDtypeStruct((n_state * hc_rows, w_b), F32)],
        scratch=scratch, value_bytes=2 * _buffer_bytes((tm, d), F32),
        name=f"even_mixer_{l}_{'carry' if carry else 'group'}")


def _odd_layer(x, hs, p, o, l, *, nb, tt, carry):
    rows, d = x.shape
    tm = nb * tt
    w_c = p["short_w"].shape[-1]
    hs_rows = SHORT_HIST * nb
    assert not carry or tm >= hs_rows

    args = [x, hs, p["norm_mix"], p["w_in_odd"], p["short_w"], p["w_out_odd"]]
    hist_spec = (lambda r, c: _const_spec((r, c))) if carry else _row_spec
    specs = [_row_spec(tm, d), hist_spec(hs_rows, w_c), _const_spec((1, d), l), _const_spec((d, 4 * w_c), o),
             _const_spec((SHORT_TAPS, w_c), o), _const_spec((w_c, d), o)]
    state_spec = (lambda r, c: pl.BlockSpec((r, c), lambda i: (0, 0))) if carry else _row_spec
    n_state = 1 if carry else rows // tm
    scratch = [pltpu.VMEM((tm, d), BF16), pltpu.VMEM((tm, w_c), F32), pltpu.VMEM((tm, w_c), BF16)]
    if carry:
        scratch.append(pltpu.VMEM((hs_rows, w_c), F32))
    return _pallas(
        functools.partial(_odd_kernel, nb=nb, tt=tt, carry=carry),
        grid=(rows // tm,), args=args, in_specs=specs,
        out_specs=[_row_spec(tm, d), state_spec(hs_rows, w_c)],
        out_shape=[jax.ShapeDtypeStruct((rows, d), F32), jax.ShapeDtypeStruct((n_state * hs_rows, w_c), F32)],
        scratch=scratch, value_bytes=2 * _buffer_bytes((tm, d), F32),
        name=f"odd_mixer_{l}_{'carry' if carry else 'group'}")


def _xattn_prompt(x, kt, vb, p, l, *, nb, tt, final_norm=None, batch_major_out=False):
    rows, d = x.shape
    tm = nb * tt
    n_pairs, d_head = nb * N_HEADS, d // N_HEADS
    assert kt.shape[1] == n_pairs and vb.shape[1] == n_pairs
    final = final_norm is not None
    args = [x, p["norm_x"], p["w_q"], kt, vb, p["w_o"]]
    specs = [_row_spec(tm, d), _const_spec((1, d), l), _const_spec((d, d), l), _const_spec(kt.shape[1:], l),
             _const_spec(vb.shape[1:], l), _const_spec((d, d), l)]
    if final:
        args, specs = args + [final_norm], specs + [_const_spec((1, d))]
    if batch_major_out:
        out_spec = pl.BlockSpec((nb, tt, d), lambda i: (0, i, 0))
        out_shape = jax.ShapeDtypeStruct((nb, rows // nb, d), F32)
    else:
        out_spec, out_shape = _row_spec(tm, d), jax.ShapeDtypeStruct((rows, d), F32)
    return _pallas(
        functools.partial(_xattn_prompt_kernel, nb=nb, tt=tt, final=final, batch_major_out=batch_major_out),
        grid=(rows // tm,), args=args, in_specs=specs, out_specs=out_spec, out_shape=out_shape,
        scratch=[pltpu.VMEM((tm, d), BF16), _chunk_major(tm, d), _chunk_major(tm, d),
                 pltpu.VMEM((n_pairs, tt, d_head), BF16), pltpu.VMEM((tm, d), F32)],
        value_bytes=4 * _buffer_bytes((tm, d), F32),
        name=f"xattn_prompt_{l}")


def _xattn_sample(x, kr, vr, p, l, *, nbs, tm, final_norm=None):
    rows, d = x.shape
    n_chunks = nbs // XATTN_SEQS
    final = final_norm is not None
    d_head = d // N_HEADS
    n_pairs = XATTN_SEQS * N_HEADS
    n_mem = kr.shape[2] * LANES // d
    cache_block = (XATTN_SEQS,) + kr.shape[2:]
    cache_spec = pl.BlockSpec((None,) + cache_block, lambda g, c: (l, g * n_chunks + c, 0, 0))
    args = [x, p["norm_x"], p["w_q"], kr, vr, p["w_o"]]
    specs = [pl.BlockSpec((tm, d), lambda g, c: (g, 0)), _const_spec((1, d), l), _const_spec((d, d), l),
             cache_spec, cache_spec, _const_spec((d, d), l)]
    if final:
        args, specs = args + [final_norm], specs + [_const_spec((1, d))]
    return _pallas(
        functools.partial(_xattn_sample_kernel, nbs=nbs, final=final),
        grid=(rows // tm, n_chunks), args=args, in_specs=specs,
        out_specs=pl.BlockSpec((tm, d), lambda g, c: (g, 0)), out_shape=jax.ShapeDtypeStruct((rows, d), F32),
        scratch=[pltpu.VMEM((tm, d), BF16), _chunk_major(tm, d), _chunk_major(tm, d), pltpu.VMEM((tm, d), F32),
                 pltpu.VMEM((n_pairs, tm // nbs, d_head), F32), pltpu.VMEM((n_pairs, n_mem, d_head), BF16),
                 pltpu.VMEM((n_pairs, n_mem, d_head), BF16)],
        value_bytes=4 * _buffer_bytes((tm, d), F32),
        name=f"xattn_sample_{l}")


def _memory_kv(mem, p, w_k, w_v, casts, n_seq, n_mem, d_head):
    d = mem.shape[-1]
    depth = w_k.shape[0]
    n_steps = depth * n_seq
    cast_specs = []
    for w in casts:
        assert w.shape[0] % (n_steps * 2 * SUBLANES) == 0
        cast_specs.append(pl.BlockSpec((w.shape[0] // n_steps, w.shape[1]), lambda l, b: (l * n_seq + b, 0)))
    r_rows = n_mem * d // LANES
    r_spec = pl.BlockSpec((None, None, r_rows, LANES), lambda l, b: (l, b, 0, 0))
    kt_spec = pl.BlockSpec((None, None, N_HEADS, d_head, n_mem), lambda l, b: (l, b, 0, 0, 0))
    vb_spec = pl.BlockSpec((None, None, N_HEADS, n_mem, d_head), lambda l, b: (l, b, 0, 0, 0))
    w_spec = pl.BlockSpec((None, d, d), lambda l, b: (l, 0, 0))
    r_shape = jax.ShapeDtypeStruct((depth, n_seq, r_rows, LANES), F32)
    return _pallas(
        functools.partial(_memory_kv_kernel, n_casts=len(casts)),
        grid=(depth, n_seq), args=[mem, p["norm_mem"], w_k, w_v, *casts],
        in_specs=[pl.BlockSpec((n_mem, d), lambda l, b: (b, 0)), pl.BlockSpec((None, 1, d), lambda l, b: (l, 0, 0)),
                  w_spec, w_spec] + cast_specs,
        out_specs=[r_spec, r_spec, kt_spec, vb_spec] + cast_specs,
        out_shape=[r_shape, r_shape, jax.ShapeDtypeStruct((depth, n_seq, N_HEADS, d_head, n_mem), BF16),
                   jax.ShapeDtypeStruct((depth, n_seq, N_HEADS, n_mem, d_head), BF16)]
        + [jax.ShapeDtypeStruct(w.shape, BF16) for w in casts],
        scratch=[pltpu.VMEM((n_mem, d), BF16), pltpu.VMEM((d, d), BF16), pltpu.VMEM((d, d), BF16)],
        value_bytes=2 * _buffer_bytes((n_mem, d), F32),
        name="memory_kv")


def _to_time_major(x, nb):
    b, t, c = x.shape
    return x.reshape(b // nb, nb, t, c).transpose(0, 2, 1, 3).reshape(b * t, c)


def _from_time_major(y, b, t, nb):
    c = y.shape[-1]
    return y.reshape(b // nb, t, nb, c).transpose(0, 2, 1, 3).reshape(b, t, c)


def _cache_rows(c):
    *lead, n_mem, heads, d_head = c.shape
    chunks = d_head // LANES
    r = c.reshape(*lead, n_mem, heads, chunks, LANES)
    return jnp.swapaxes(r, -3, -2).reshape(*lead, n_mem * chunks * heads, LANES)


def _cache_from_rows(r, n_mem, heads, d_head):
    *lead, _, _ = r.shape
    chunks = d_head // LANES
    c = r.reshape(*lead, n_mem, chunks, heads, LANES)
    return jnp.swapaxes(c, -3, -2).reshape(*lead, n_mem, heads, d_head)


def kernel(x_prompt, x_sample, state_pool, state_conv, state_short, cache_mem_k, cache_mem_v, mem_prompt, norm_mix, w_in_even, pool_w, pool_scale, conv_w, conv_b, ln_g, ln_b, w_out_even, w_in_odd, short_w, w_out_odd, norm_x, norm_mem, w_q, w_k, w_v, w_o, final_norm):
    b_p, t_p, d = x_prompt.shape
    b_s, t_s, _ = x_sample.shape
    depth = norm_mix.shape[0]
    n_mem = mem_prompt.shape[1]
    heads, d_head = cache_mem_k.shape[-2:]
    assert heads == N_HEADS and d_head % LANES == 0 and depth >= 2
    assert d_head & (d_head - 1) == 0 and d_head.bit_length() % 2 == 1, "the score scale must be a power of two"

    nb_p, tt_p, tt_a = b_p, TILE_ROWS // b_p, ATTN_TILE_ROWS // b_p
    nb_s, tt_s = SAMPLE_TILE_ROWS // t_s, t_s
    for nb in (nb_p, nb_s):
        assert nb % SUBLANES == 0 and nb & (nb - 1) == 0
    assert t_p % tt_p == 0 and t_p % tt_a == 0 and b_s % nb_s == 0 and nb_s % XATTN_SEQS == 0

    row = lambda a: a.reshape(a.shape[0], 1, a.shape[-1])
    p = {
        "norm_mix": row(norm_mix), "norm_x": row(norm_x), "norm_mem": row(norm_mem),
        "pool_scale": row(pool_scale), "conv_w": conv_w, "conv_b": row(conv_b), "ln_g": row(ln_g), "ln_b": row(ln_b),
        "short_w": short_w,
    }
    fnorm = final_norm.reshape(1, d)
    w_a, w_b, w_c = pool_scale.shape[-1], conv_b.shape[-1], short_w.shape[-1]

    trunk = {"w_in_even": w_in_even, "pool_w": pool_w, "w_out_even": w_out_even, "w_in_odd": w_in_odd,
             "w_out_odd": w_out_odd, "w_q": w_q, "w_o": w_o}
    kr, vr, kt, vb, *cast = _memory_kv(mem_prompt.reshape(b_p * n_mem, d), p, w_k, w_v,
                                       [w.reshape(-1, w.shape[-1]) for w in trunk.values()], b_p, n_mem, d_head)
    p.update({name: c.reshape(w.shape) for (name, w), c in zip(trunk.items(), cast)})
    mem_k_p = _cache_from_rows(kr, n_mem, heads, d_head)
    mem_v_p = _cache_from_rows(vr, n_mem, heads, d_head)
    kt = kt.reshape(depth, b_p * heads, d_head, n_mem)
    vb = vb.reshape(depth, b_p * heads, n_mem, d_head)

    xp = x_prompt
    zp = jnp.zeros((POOL_HIST * nb_p, w_a), F32)
    zc = jnp.zeros((CONV_HIST * nb_p, w_b), F32)
    zs = jnp.zeros((SHORT_HIST * nb_p, w_c), F32)
    pool_p, conv_p, short_p = [], [], []
    for l in range(depth):
        last = l == depth - 1
        if l % 2 == 0:
            xp, sp, sc = _even_layer(xp, zp, zc, p, l // 2, l, nb=nb_p, tt=tt_p, start_pos=0, carry=True)
            pool_p.append(_from_time_major(sp, b_p, POOL_HIST, nb_p))
            conv_p.append(_from_time_major(sc, b_p, CONV_HIST, nb_p))
        else:
            xp, ss = _odd_layer(xp, zs, p, l // 2, l, nb=nb_p, tt=tt_p, carry=True)
            short_p.append(_from_time_major(ss, b_p, SHORT_HIST, nb_p))
        xp = _xattn_prompt(xp, kt, vb, p, l, nb=nb_p, tt=tt_a, final_norm=fnorm if last else None,
                           batch_major_out=last)
    y_prompt = xp

    krs, vrs = _cache_rows(cache_mem_k), _cache_rows(cache_mem_v)
    xs = _to_time_major(x_sample, nb_s)
    pool_s, conv_s, short_s = [], [], []
    for l in range(depth):
        if l % 2 == 0:
            e = l // 2
            xs, sp, sc = _even_layer(xs, _to_time_major(state_pool[e], nb_s), _to_time_major(state_conv[e], nb_s),
                                     p, e, l, nb=nb_s, tt=tt_s, start_pos=PAST_LEN, carry=False)
            pool_s.append(_from_time_major(sp, b_s, POOL_HIST, nb_s))
            conv_s.append(_from_time_major(sc, b_s, CONV_HIST, nb_s))
        else:
            o = l // 2
            xs, ss = _odd_layer(xs, _to_time_major(state_short[o], nb_s), p, o, l, nb=nb_s, tt=tt_s, carry=False)
            short_s.append(_from_time_major(ss, b_s, SHORT_HIST, nb_s))
        xs = _xattn_sample(xs, krs, vrs, p, l, nbs=nb_s, tm=nb_s * tt_s,
                           final_norm=fnorm if l == depth - 1 else None)
    y_sample = _from_time_major(xs, b_s, t_s, nb_s)

    return (y_prompt, y_sample, jnp.stack(pool_p), jnp.stack(conv_p), jnp.stack(short_p), mem_k_p, mem_v_p,
            jnp.stack(pool_s), jnp.stack(conv_s), jnp.stack(short_s))
```

```python
import functools
import math

import jax
import jax.numpy as jnp
from jax import lax
from jax.experimental import pallas as pl
from jax.experimental.pallas import tpu as pltpu

F32 = jnp.float32
BF16 = jnp.bfloat16

EPS = 1e-6
POOL_WINDOWS = (2, 4, 8, 16)
POOL_HIST = max(POOL_WINDOWS) - 1
CONV_TAPS = 31
CONV_HIST = CONV_TAPS - 1
SHORT_TAPS = 3
SHORT_HIST = SHORT_TAPS - 1
N_HEADS = 4
PAST_LEN = 16384

LANES = 128
SUBLANES = 8
MXU_COLS = 256
VMEM_BYTES = 64 * 1024 * 1024

TILE_ROWS = 512
SAMPLE_TILE_ROWS = 256
ATTN_TILE_ROWS = 1024
NORM_ROWS = 32
CONV_ROWS = 128
CONV_TAP_GROUP = 8
XATTN_SEQS = 8


def _mm(a, b):
    return jnp.dot(a, b, preferred_element_type=F32)


def _silu(x):
    return x * jax.nn.sigmoid(x)


def _rmsnorm(x, g):
    ms = jnp.mean(x * x, axis=-1, keepdims=True)
    return x * lax.rsqrt(ms + EPS) * g


def _softmax(s):
    e = jnp.exp(s - jnp.max(s, axis=-1, keepdims=True))
    return e / jnp.sum(e, axis=-1, keepdims=True)


def _norm_rows(dst_ref, src_ref, g_ref, dtype):
    rows = src_ref.shape[0]
    for r0 in range(0, rows, NORM_ROWS):
        rs = slice(r0, r0 + NORM_ROWS)
        dst_ref[rs, :] = _rmsnorm(src_ref[rs, :], g_ref[...]).astype(dtype)


def _window_rows(hist_ref, new_ref, start, n_rows, cols):
    hist_rows = hist_ref.shape[0]
    n_hist = min(max(hist_rows - start, 0), n_rows)
    parts = []
    if n_hist > 0:
        parts.append(hist_ref[start:start + n_hist, cols])
    if n_hist < n_rows:
        s = start + n_hist - hist_rows
        parts.append(new_ref[s:s + n_rows - n_hist, cols])
    return parts[0] if len(parts) == 1 else jnp.concatenate(parts, axis=0)


def _last_rows(dst_ref, hist_ref, new_ref):
    n, tm = dst_ref.shape[0], new_ref.shape[0]
    if tm >= n:
        dst_ref[...] = new_ref[tm - n:tm, :]
    else:
        dst_ref[0:n - tm, :] = hist_ref[tm:n, :]
        dst_ref[n - tm:n, :] = new_ref[...]


def _exact_zero_of(x):
    bits = lax.bitcast_convert_type(x, jnp.uint32)
    sixteen = jnp.full(bits.shape, 16, jnp.uint32)
    bits = lax.shift_right_logical(lax.shift_right_logical(bits, sixteen), sixteen)
    return lax.bitcast_convert_type(bits, F32)


def _lane_chunks(width):
    return [slice(c0, c0 + LANES) for c0 in range(0, width, LANES)]


def _even_kernel(*refs, nb, tt, start_pos, carry, batch_major_in):
    tm = nb * tt
    it = iter(refs)
    x_ref, hp_ref, hc_ref, nmix_ref, win_ref, poolw_ref, pscale_ref = (next(it) for _ in range(7))
    convw_ref, convb_ref, lng_ref, lnb_ref, wout_ref = (next(it) for _ in range(5))
    o_ref, newp_ref, newc_ref = next(it), next(it), next(it)
    h_scr, pool_scr, vnew_scr, ag_scr, bg_scr, c_scr, ac_scr, wtile_scr = (next(it) for _ in range(8))
    hc_scr = next(it) if carry else None
    if batch_major_in:
        xc_scr, xt_scr = next(it), next(it)

    i = pl.program_id(0)
    n_steps = pl.num_programs(0)
    w_a = pscale_ref.shape[-1]
    w_b = convb_ref.shape[-1]
    hp_rows = POOL_HIST * nb
    pool_group = w_a // len(POOL_WINDOWS)

    if carry:
        @pl.when(i == 0)
        def _():
            pool_scr[0:hp_rows, :] = hp_ref[...]
            hc_scr[...] = hc_ref[...]
        hist_c = hc_scr
    else:
        pool_scr[0:hp_rows, :] = hp_ref[...]
        hist_c = hc_ref

    if batch_major_in:
        for b in range(nb):
            for j, cs in enumerate(_lane_chunks(x_ref.shape[-1])):
                xc_scr[j, pl.ds(b, tt, stride=nb), :] = x_ref[b, :, cs]
        for j, cs in enumerate(_lane_chunks(x_ref.shape[-1])):
            xt_scr[:, cs] = xc_scr[j]
        x_src = xt_scr
    else:
        x_src = x_ref

    _norm_rows(h_scr, x_src, nmix_ref, BF16)

    def column_piece(col0, c0):
        return _mm(h_scr[...], win_ref[:, col0 + c0:col0 + c0 + MXU_COLS])
    for c0 in range(0, w_b, MXU_COLS):
        vnew_scr[:, c0:c0 + MXU_COLS] = column_piece(2 * w_a, c0) * jax.nn.sigmoid(column_piece(2 * w_a + w_b, c0))

    def project(dst_ref, dst_rows, col0, c0, act):
        def run():
            dst_ref[dst_rows, c0:c0 + MXU_COLS] = act(column_piece(col0, c0))
        return run
    pieces = ([project(pool_scr, slice(hp_rows, hp_rows + tm), 0, c0, lambda z: z) for c0 in range(0, w_a, MXU_COLS)]
              + [project(ag_scr, slice(None), w_a, c0, _silu) for c0 in range(0, w_a, MXU_COLS)]
              + [project(bg_scr, slice(None), 2 * w_a + 2 * w_b, c0, _silu) for c0 in range(0, w_b, MXU_COLS)])
    conv_chunks = _lane_chunks(w_b)
    for k in range(CONV_TAPS):
        wtile_scr[k * SUBLANES:(k + 1) * SUBLANES, :] = jnp.broadcast_to(convw_ref[k:k + 1, :], (SUBLANES, w_b))

    def conv_chunk(n, cs):
        order = None
        for r0 in range(0, tm, CONV_ROWS):
            bias = jnp.broadcast_to(convb_ref[:, cs], (SUBLANES, LANES))
            if order is not None:
                bias = bias + order
            acc = jnp.concatenate([bias] * (CONV_ROWS // SUBLANES), axis=0)
            for k0 in range(0, CONV_TAPS, CONV_TAP_GROUP):
                taps = range(k0, min(k0 + CONV_TAP_GROUP, CONV_TAPS))
                win = _window_rows(hist_c, vnew_scr, r0 + k0 * nb, (len(taps) - 1) * nb + CONV_ROWS, cs)
                for k in taps:
                    wk = wtile_scr[k * SUBLANES:(k + 1) * SUBLANES, cs]
                    acc = acc + (win[(k - k0) * nb:(k - k0) * nb + CONV_ROWS, :]
                                 * jnp.concatenate([wk] * (CONV_ROWS // SUBLANES), axis=0))
            c_scr[r0:r0 + CONV_ROWS, cs] = acc
            order = _exact_zero_of(acc[0:SUBLANES, :])
        for run in pieces[n * len(pieces) // len(conv_chunks):(n + 1) * len(pieces) // len(conv_chunks)]:
            run()

    for n, cs in enumerate(conv_chunks):
        pl.when(i >= 0)(functools.partial(conv_chunk, n, cs))

    row = lax.broadcasted_iota(jnp.int32, (tm, LANES), 0)
    step = row >> (nb.bit_length() - 1)
    pos1 = step + (start_pos + 1 + (i * tt if carry else 0))
    for g, w in enumerate(POOL_WINDOWS):
        inv_cnt = 1.0 / jnp.minimum(pos1, w).astype(F32)
        a_parts = []
        for c0 in range(g * pool_group, (g + 1) * pool_group, LANES):
            e = pool_scr[(POOL_HIST + 1 - w) * nb:hp_rows + tm, c0:c0 + LANES]
            s, sh = e, 1
            while sh < w:
                n = s.shape[0]
                s = s[sh * nb:, :] + s[:n - sh * nb, :]
                sh *= 2
            a_parts.append((s * inv_cnt - e[(w - 1) * nb:, :]).astype(BF16))
        cs = slice(g * pool_group, (g + 1) * pool_group)
        pa = _mm(jnp.concatenate(a_parts, axis=1), poolw_ref[g])
        ac_scr[:, cs] = (pa * pscale_ref[:, cs] * ag_scr[:, cs]).astype(BF16)

    out_pieces = [slice(c0, c0 + MXU_COLS) for c0 in range(0, o_ref.shape[-1], MXU_COLS)]
    for cs in out_pieces:
        o_ref[:, cs] = x_src[:, cs] + _mm(ac_scr[:, 0:w_a], wout_ref[0:w_a, cs])

    for r0 in range(0, tm, NORM_ROWS):
        rs = slice(r0, r0 + NORM_ROWS)
        c = c_scr[rs, :]
        d = c - jnp.mean(c, axis=-1, keepdims=True)
        var = jnp.mean(d * d, axis=-1, keepdims=True)
        y = d * lax.rsqrt(var + EPS) * lng_ref[...] + lnb_ref[...]
        ac_scr[rs, w_a:w_a + w_b] = (_silu(y) * bg_scr[rs, :]).astype(BF16)

    for cs in out_pieces:
        o_ref[:, cs] += _mm(ac_scr[:, w_a:w_a + w_b], wout_ref[w_a:w_a + w_b, cs])

    if carry:
        _last_rows(hc_scr, hc_scr, vnew_scr)
        hp_new = pool_scr[tm:tm + hp_rows, :]
        pool_scr[0:hp_rows, :] = hp_new

        @pl.when(i == n_steps - 1)
        def _():
            newp_ref[...] = pool_scr[0:hp_rows, :]
            newc_ref[...] = hc_scr[...]
    else:
        newp_ref[...] = pool_scr[tm:tm + hp_rows, :]
        _last_rows(newc_ref, hc_ref, vnew_scr)


def _odd_kernel(*refs, nb, tt, carry):
    tm = nb * tt
    it = iter(refs)
    x_ref, hs_ref, nmix_ref, win_ref, shortw_ref, wout_ref = (next(it) for _ in range(6))
    o_ref, news_ref = next(it), next(it)
    h_scr, unew_scr, y_scr = (next(it) for _ in range(3))
    hs_scr = next(it) if carry else None

    i = pl.program_id(0)
    n_steps = pl.num_programs(0)
    w_c = shortw_ref.shape[-1]
    d_model = x_ref.shape[-1]

    if carry:
        @pl.when(i == 0)
        def _():
            hs_scr[...] = hs_ref[...]
        hist = hs_scr
    else:
        hist = hs_ref

    _norm_rows(h_scr, x_ref, nmix_ref, BF16)

    def column_piece(col0, c0):
        return _mm(h_scr[...], win_ref[:, col0 + c0:col0 + c0 + MXU_COLS])
    for c0 in range(0, w_c, MXU_COLS):
        unew_scr[:, c0:c0 + MXU_COLS] = column_piece(w_c, c0) * column_piece(2 * w_c, c0)
    for c0 in range(0, w_c, MXU_COLS):
        gate = column_piece(0, c0) * _silu(column_piece(3 * w_c, c0))
        for c1 in range(0, MXU_COLS, LANES):
            cs = slice(c0 + c1, c0 + c1 + LANES)
            acc = _window_rows(hist, unew_scr, 0, tm, cs) * shortw_ref[0:1, cs]
            for k in range(1, SHORT_TAPS):
                acc = acc + _window_rows(hist, unew_scr, k * nb, tm, cs) * shortw_ref[k:k + 1, cs]
            y_scr[:, cs] = (gate[:, c1:c1 + LANES] * acc).astype(BF16)

    for c0 in range(0, d_model, MXU_COLS):
        cs = slice(c0, c0 + MXU_COLS)
        o_ref[:, cs] = x_ref[:, cs] + _mm(y_scr[...], wout_ref[:, cs])

    if carry:
        _last_rows(hs_scr, hs_scr, unew_scr)

        @pl.when(i == n_steps - 1)
        def _():
            news_ref[...] = hs_scr[...]
    else:
        _last_rows(news_ref, hs_ref, unew_scr)


def _project_q(h_scr, wq_ref, q_scr, d_head):
    chunks = d_head // LANES
    for hd in range(N_HEADS):
        q = _mm(h_scr[...], wq_ref[:, hd * d_head:(hd + 1) * d_head]) * d_head ** -0.5
        for c in range(chunks):
            q_scr[hd * chunks + c] = q[:, c * LANES:(c + 1) * LANES]


def _project_out(dst_ref, x_ref, oh_scr, ob_scr, wo_ref):
    for j in range(oh_scr.shape[0]):
        ob_scr[:, j * LANES:(j + 1) * LANES] = oh_scr[j].astype(BF16)
    for c0 in range(0, dst_ref.shape[-1], MXU_COLS):
        cs = slice(c0, c0 + MXU_COLS)
        dst_ref[:, cs] = x_ref[:, cs] + _mm(ob_scr[...], wo_ref[:, cs])


def _pair_slot(seq, j, chunks):
    return seq * N_HEADS + j // chunks, slice((j % chunks) * LANES, (j % chunks + 1) * LANES)


def _xattn_prompt_kernel(*refs, nb, tt, final, batch_major_out):
    it = iter(refs)
    x_ref, nx_ref, wq_ref, kt_ref, v_ref, wo_ref = (next(it) for _ in range(6))
    fn_ref = next(it) if final else None
    o_ref = next(it)
    h_scr, q_scr, oh_scr, qg_scr, y_scr = (next(it) for _ in range(5))
    d_head = v_ref.shape[-1]
    chunks = d_head // LANES
    n_chunks = N_HEADS * chunks

    _norm_rows(h_scr, x_ref, nx_ref, BF16)
    _project_q(h_scr, wq_ref, q_scr, d_head)
    for b in range(nb):
        for j in range(n_chunks):
            pair, cs = _pair_slot(b, j, chunks)
            qg_scr[pair, :, cs] = q_scr[j, pl.ds(b, tt, stride=nb), :].astype(BF16)
    s = jnp.einsum("pqd,pdk->pqk", qg_scr[...], kt_ref[...], preferred_element_type=F32)
    o = jnp.einsum("pqk,pkd->pqd", _softmax(s).astype(BF16), v_ref[...], preferred_element_type=F32)
    for b in range(nb):
        for j in range(n_chunks):
            pair, cs = _pair_slot(b, j, chunks)
            oh_scr[j, pl.ds(b, tt, stride=nb), :] = o[pair, :, cs]
    dst = y_scr if (final or batch_major_out) else o_ref
    _project_out(dst, x_ref, oh_scr, h_scr, wo_ref)
    if not (final or batch_major_out):
        return
    if final:
        _norm_rows(y_scr, y_scr, fn_ref, F32)
    if not batch_major_out:
        o_ref[...] = y_scr[...]
        return
    for j, cs in enumerate(_lane_chunks(y_scr.shape[-1])):
        q_scr[j] = y_scr[:, cs]
    for b in range(nb):
        for j, cs in enumerate(_lane_chunks(y_scr.shape[-1])):
            o_ref[b, :, cs] = q_scr[j, pl.ds(b, tt, stride=nb), :]


def _xattn_sample_kernel(*refs, nbs, final):
    it = iter(refs)
    x_ref, nx_ref, wq_ref, k_ref, v_ref, wo_ref = (next(it) for _ in range(6))
    fn_ref = next(it) if final else None
    o_ref = next(it)
    h_scr, q_scr, oh_scr, y_scr, qg_scr, kg_scr, vg_scr = (next(it) for _ in range(7))

    c = pl.program_id(1)
    tt = x_ref.shape[0] // nbs
    seqs = k_ref.shape[0]
    d_head = wq_ref.shape[-1] // N_HEADS
    chunks = d_head // LANES
    n_chunks = N_HEADS * chunks
    n_mem = k_ref.shape[1] // n_chunks

    @pl.when(c == 0)
    def _():
        _norm_rows(h_scr, x_ref, nx_ref, BF16)
        _project_q(h_scr, wq_ref, q_scr, d_head)

    for bl in range(seqs):
        rows = pl.ds(c * seqs + bl, tt, stride=nbs)
        for j in range(n_chunks):
            pair, cs = _pair_slot(bl, j, chunks)
            cache_rows = pl.ds((j % chunks) * N_HEADS + j // chunks, n_mem, stride=n_chunks)
            qg_scr[pair, :, cs] = q_scr[j, rows, :]
            kg_scr[pair, :, cs] = k_ref[bl, cache_rows, :].astype(BF16)
            vg_scr[pair, :, cs] = v_ref[bl, cache_rows, :].astype(BF16)
    s = jnp.einsum("pqd,pkd->pqk", qg_scr[...].astype(BF16), kg_scr[...], preferred_element_type=F32)
    o = jnp.einsum("pqk,pkd->pqd", _softmax(s).astype(BF16), vg_scr[...], preferred_element_type=F32)
    for bl in range(seqs):
        rows = pl.ds(c * seqs + bl, tt, stride=nbs)
        for j in range(n_chunks):
            pair, cs = _pair_slot(bl, j, chunks)
            oh_scr[j, rows, :] = o[pair, :, cs]

    @pl.when(c == pl.num_programs(1) - 1)
    def _():
        if final:
            _project_out(y_scr, x_ref, oh_scr, h_scr, wo_ref)
            _norm_rows(o_ref, y_scr, fn_ref, F32)
        else:
            _project_out(o_ref, x_ref, oh_scr, h_scr, wo_ref)


def _memory_kv_kernel(*refs, n_casts):
    mem_ref, nmem_ref, wk_ref, wv_ref = refs[:4]
    cast_in = refs[4:4 + n_casts]
    kr_ref, vr_ref, kt_ref, vb_ref = refs[4 + n_casts:8 + n_casts]
    cast_out = refs[8 + n_casts:8 + 2 * n_casts]
    h_scr, wkb_scr, wvb_scr = refs[8 + 2 * n_casts:]
    for src, dst in zip(cast_in, cast_out):
        dst[...] = src[...].astype(BF16)

    @pl.when(pl.program_id(1) == 0)
    def _():
        wkb_scr[...] = wk_ref[...].astype(BF16)
        wvb_scr[...] = wv_ref[...].astype(BF16)

    n_mem = mem_ref.shape[0]
    d_head = vb_ref.shape[-1]
    chunks = d_head // LANES
    n_chunks = chunks * N_HEADS
    _norm_rows(h_scr, mem_ref, nmem_ref, BF16)
    for w_ref, r_ref in ((wkb_scr, kr_ref), (wvb_scr, vr_ref)):
        for hd in range(N_HEADS):
            kh = _mm(h_scr[...], w_ref[:, hd * d_head:(hd + 1) * d_head])
            if r_ref is kr_ref:
                kt_ref[hd] = kh.T.astype(BF16)
            else:
                vb_ref[hd] = kh.astype(BF16)
            for cc in range(chunks):
                r_ref[pl.ds(cc * N_HEADS + hd, n_mem, stride=n_chunks), :] = kh[:, cc * LANES:(cc + 1) * LANES]


def _const_spec(shape, layer=None):
    if layer is None:
        return pl.BlockSpec(shape, lambda *_: (0,) * len(shape), pipeline_mode=pl.Buffered(1))
    return pl.BlockSpec((None,) + shape, lambda *_: (layer,) + (0,) * len(shape), pipeline_mode=pl.Buffered(1))


def _row_spec(rows, cols):
    return pl.BlockSpec((rows, cols), lambda i: (i, 0))


def _buffer_bytes(shape, dtype):
    itemsize = jnp.dtype(dtype).itemsize
    dims = [s for s in shape if s is not None]
    rows = SUBLANES * 4 // itemsize
    if len(dims) >= 2:
        dims[-2] = -(-dims[-2] // rows) * rows
    dims[-1] = -(-dims[-1] // LANES) * LANES
    return math.prod(dims) * itemsize


def _pallas(body, *, grid, args, in_specs, out_specs, out_shape, scratch, value_bytes, name):
    single = not isinstance(out_specs, (list, tuple))
    o_specs, o_shapes = ([out_specs], [out_shape]) if single else (out_specs, out_shape)
    buffers = lambda spec: spec.pipeline_mode.buffer_count if spec.pipeline_mode is not None else 2
    limit = (sum(_buffer_bytes(s.block_shape, a.dtype) * buffers(s) for s, a in zip(in_specs, args))
             + sum(_buffer_bytes(s.block_shape, o.dtype) * buffers(s) for s, o in zip(o_specs, o_shapes))
             + sum(_buffer_bytes(m.shape, m.dtype) for m in scratch) + value_bytes)
    assert limit <= VMEM_BYTES, (name, limit)
    return pl.pallas_call(
        body, grid=grid, in_specs=in_specs, out_specs=out_specs, out_shape=out_shape, scratch_shapes=scratch,
        compiler_params=pltpu.CompilerParams(dimension_semantics=("arbitrary",) * len(grid), vmem_limit_bytes=limit),
        name=name)(*args)


def _chunk_major(tm, d):
    return pltpu.VMEM((d // LANES, tm, LANES), F32)


def _even_layer(x, hp, hc, p, e, l, *, nb, tt, start_pos, carry):
    batch_major_in = x.ndim == 3
    d = x.shape[-1]
    tm = nb * tt
    rows = x.shape[0] * x.shape[1] if batch_major_in else x.shape[0]
    w_a, w_b = p["pool_scale"].shape[-1], p["conv_b"].shape[-1]
    hp_rows, hc_rows = POOL_HIST * nb, CONV_HIST * nb
    assert not carry or tm >= hc_rows
    assert not batch_major_in or (carry and x.shape[0] == nb)
    n_groups = len(POOL_WINDOWS)

    args = [x, hp, hc, p["norm_mix"], p["w_in_even"], p["pool_w"], p["pool_scale"], p["conv_w"], p["conv_b"],
            p["ln_g"], p["ln_b"], p["w_out_even"]]
    hist_spec = (lambda r, c: _const_spec((r, c))) if carry else _row_spec
    x_spec = pl.BlockSpec((nb, tt, d), lambda i: (0, i, 0)) if batch_major_in else _row_spec(tm, d)
    specs = [x_spec, hist_spec(hp_rows, w_a), hist_spec(hc_rows, w_b),
             _const_spec((1, d), l), _const_spec((d, 2 * w_a + 3 * w_b), e),
             _const_spec((n_groups, w_a // n_groups, w_a // n_groups), e), _const_spec((1, w_a), e),
             _const_spec((CONV_TAPS, w_b), e), _const_spec((1, w_b), e), _const_spec((1, w_b), e),
             _const_spec((1, w_b), e), _const_spec((w_a + w_b, d), e)]

    state_spec = (lambda r, c: pl.BlockSpec((r, c), lambda i: (0, 0))) if carry else _row_spec
    n_state = 1 if carry else rows // tm
    scratch = [pltpu.VMEM((tm, d), BF16), pltpu.VMEM((hp_rows + tm, w_a), F32), pltpu.VMEM((tm, w_b), F32),
               pltpu.VMEM((tm, w_a), F32), pltpu.VMEM((tm, w_b), F32), pltpu.VMEM((tm, w_b), F32),
               pltpu.VMEM((tm, w_a + w_b), BF16), pltpu.VMEM((CONV_TAPS * SUBLANES, w_b), F32)]
    if carry:
        scratch.append(pltpu.VMEM((hc_rows, w_b), F32))
    if batch_major_in:
        scratch += [_chunk_major(tm, d), pltpu.VMEM((tm, d), F32)]
    return _pallas(
        functools.partial(_even_kernel, nb=nb, tt=tt, start_pos=start_pos, carry=carry,
                          batch_major_in=batch_major_in),
        grid=(rows // tm,), args=args, in_specs=specs,
        out_specs=[_row_spec(tm, d), state_spec(hp_rows, w_a), state_spec(hc_rows, w_b)],
        out_shape=[jax.ShapeDtypeStruct((rows, d), F32), jax.ShapeDtypeStruct((n_state * hp_rows, w_a), F32),
                   jax.ShapeDtypeStruct((n_state * hc_rows, w_b), F32)],
        scratch=scratch, value_bytes=2 * _buffer_bytes((tm, d), F32),
        name=f"even_mixer_{l}_{'carry' if carry else 'group'}")


def _odd_layer(x, hs, p, o, l, *, nb, tt, carry):
    rows, d = x.shape
    tm = nb * tt
    w_c = p["short_w"].shape[-1]
    hs_rows = SHORT_HIST * nb
    assert not carry or tm >= hs_rows

    args = [x, hs, p["norm_mix"], p["w_in_odd"], p["short_w"], p["w_out_odd"]]
    hist_spec = (lambda r, c: _const_spec((r, c))) if carry else _row_spec
    specs = [_row_spec(tm, d), hist_spec(hs_rows, w_c), _const_spec((1, d), l), _const_spec((d, 4 * w_c), o),
             _const_spec((SHORT_TAPS, w_c), o), _const_spec((w_c, d), o)]
    state_spec = (lambda r, c: pl.BlockSpec((r, c), lambda i: (0, 0))) if carry else _row_spec
    n_state = 1 if carry else rows // tm
    scratch = [pltpu.VMEM((tm, d), BF16), pltpu.VMEM((tm, w_c), F32), pltpu.VMEM((tm, w_c), BF16)]
    if carry:
        scratch.append(pltpu.VMEM((hs_rows, w_c), F32))
    return _pallas(
        functools.partial(_odd_kernel, nb=nb, tt=tt, carry=carry),
        grid=(rows // tm,), args=args, in_specs=specs,
        out_specs=[_row_spec(tm, d), state_spec(hs_rows, w_c)],
        out_shape=[jax.ShapeDtypeStruct((rows, d), F32), jax.ShapeDtypeStruct((n_state * hs_rows, w_c), F32)],
        scratch=scratch, value_bytes=2 * _buffer_bytes((tm, d), F32),
        name=f"odd_mixer_{l}_{'carry' if carry else 'group'}")


def _xattn_prompt(x, kt, vb, p, l, *, nb, tt, final_norm=None, batch_major_out=False):
    rows, d = x.shape
    tm = nb * tt
    n_pairs, d_head = nb * N_HEADS, d // N_HEADS
    assert kt.shape[1] == n_pairs and vb.shape[1] == n_pairs
    final = final_norm is not None
    args = [x, p["norm_x"], p["w_q"], kt, vb, p["w_o"]]
    specs = [_row_spec(tm, d), _const_spec((1, d), l), _const_spec((d, d), l), _const_spec(kt.shape[1:], l),
             _const_spec(vb.shape[1:], l), _const_spec((d, d), l)]
    if final:
        args, specs = args + [final_norm], specs + [_const_spec((1, d))]
    if batch_major_out:
        out_spec = pl.BlockSpec((nb, tt, d), lambda i: (0, i, 0))
        out_shape = jax.ShapeDtypeStruct((nb, rows // nb, d), F32)
    else:
        out_spec, out_shape = _row_spec(tm, d), jax.ShapeDtypeStruct((rows, d), F32)
    return _pallas(
        functools.partial(_xattn_prompt_kernel, nb=nb, tt=tt, final=final, batch_major_out=batch_major_out),
        grid=(rows // tm,), args=args, in_specs=specs, out_specs=out_spec, out_shape=out_shape,
        scratch=[pltpu.VMEM((tm, d), BF16), _chunk_major(tm, d), _chunk_major(tm, d),
                 pltpu.VMEM((n_pairs, tt, d_head), BF16), pltpu.VMEM((tm, d), F32)],
        value_bytes=4 * _buffer_bytes((tm, d), F32),
        name=f"xattn_prompt_{l}")


def _xattn_sample(x, kr, vr, p, l, *, nbs, tm, final_norm=None):
    rows, d = x.shape
    n_chunks = nbs // XATTN_SEQS
    final = final_norm is not None
    d_head = d // N_HEADS
    n_pairs = XATTN_SEQS * N_HEADS
    n_mem = kr.shape[2] * LANES // d
    cache_block = (XATTN_SEQS,) + kr.shape[2:]
    cache_spec = pl.BlockSpec((None,) + cache_block, lambda g, c: (l, g * n_chunks + c, 0, 0))
    args = [x, p["norm_x"], p["w_q"], kr, vr, p["w_o"]]
    specs = [pl.BlockSpec((tm, d), lambda g, c: (g, 0)), _const_spec((1, d), l), _const_spec((d, d), l),
             cache_spec, cache_spec, _const_spec((d, d), l)]
    if final:
        args, specs = args + [final_norm], specs + [_const_spec((1, d))]
    return _pallas(
        functools.partial(_xattn_sample_kernel, nbs=nbs, final=final),
        grid=(rows // tm, n_chunks), args=args, in_specs=specs,
        out_specs=pl.BlockSpec((tm, d), lambda g, c: (g, 0)), out_shape=jax.ShapeDtypeStruct((rows, d), F32),
        scratch=[pltpu.VMEM((tm, d), BF16), _chunk_major(tm, d), _chunk_major(tm, d), pltpu.VMEM((tm, d), F32),
                 pltpu.VMEM((n_pairs, tm // nbs, d_head), F32), pltpu.VMEM((n_pairs, n_mem, d_head), BF16),
                 pltpu.VMEM((n_pairs, n_mem, d_head), BF16)],
        value_bytes=4 * _buffer_bytes((tm, d), F32),
        name=f"xattn_sample_{l}")


def _memory_kv(mem, p, w_k, w_v, casts, n_seq, n_mem, d_head):
    d = mem.shape[-1]
    depth = w_k.shape[0]
    n_steps = depth * n_seq
    cast_specs = []
    for w in casts:
        assert w.shape[0] % (n_steps * 2 * SUBLANES) == 0
        cast_specs.append(pl.BlockSpec((w.shape[0] // n_steps, w.shape[1]), lambda l, b: (l * n_seq + b, 0)))
    r_rows = n_mem * d // LANES
    r_spec = pl.BlockSpec((None, None, r_rows, LANES), lambda l, b: (l, b, 0, 0))
    kt_spec = pl.BlockSpec((None, None, N_HEADS, d_head, n_mem), lambda l, b: (l, b, 0, 0, 0))
    vb_spec = pl.BlockSpec((None, None, N_HEADS, n_mem, d_head), lambda l, b: (l, b, 0, 0, 0))
    w_spec = pl.BlockSpec((None, d, d), lambda l, b: (l, 0, 0))
    r_shape = jax.ShapeDtypeStruct((depth, n_seq, r_rows, LANES), F32)
    return _pallas(
        functools.partial(_memory_kv_kernel, n_casts=len(casts)),
        grid=(depth, n_seq), args=[mem, p["norm_mem"], w_k, w_v, *casts],
        in_specs=[pl.BlockSpec((n_mem, d), lambda l, b: (b, 0)), pl.BlockSpec((None, 1, d), lambda l, b: (l, 0, 0)),
                  w_spec, w_spec] + cast_specs,
        out_specs=[r_spec, r_spec, kt_spec, vb_spec] + cast_specs,
        out_shape=[r_shape, r_shape, jax.ShapeDtypeStruct((depth, n_seq, N_HEADS, d_head, n_mem), BF16),
                   jax.ShapeDtypeStruct((depth, n_seq, N_HEADS, n_mem, d_head), BF16)]
        + [jax.ShapeDtypeStruct(w.shape, BF16) for w in casts],
        scratch=[pltpu.VMEM((n_mem, d), BF16), pltpu.VMEM((d, d), BF16), pltpu.VMEM((d, d), BF16)],
        value_bytes=2 * _buffer_bytes((n_mem, d), F32),
        name="memory_kv")


def _to_time_major(x, nb):
    b, t, c = x.shape
    return x.reshape(b // nb, nb, t, c).transpose(0, 2, 1, 3).reshape(b * t, c)


def _from_time_major(y, b, t, nb):
    c = y.shape[-1]
    return y.reshape(b // nb, t, nb, c).transpose(0, 2, 1, 3).reshape(b, t, c)


def _cache_rows(c):
    *lead, n_mem, heads, d_head = c.shape
    chunks = d_head // LANES
    r = c.reshape(*lead, n_mem, heads, chunks, LANES)
    return jnp.swapaxes(r, -3, -2).reshape(*lead, n_mem * chunks * heads, LANES)


def _cache_from_rows(r, n_mem, heads, d_head):
    *lead, _, _ = r.shape
    chunks = d_head // LANES
    c = r.reshape(*lead, n_mem, chunks, heads, LANES)
    return jnp.swapaxes(c, -3, -2).reshape(*lead, n_mem, heads, d_head)


def kernel(x_prompt, x_sample, state_pool, state_conv, state_short, cache_mem_k, cache_mem_v, mem_prompt, norm_mix, w_in_even, pool_w, pool_scale, conv_w, conv_b, ln_g, ln_b, w_out_even, w_in_odd, short_w, w_out_odd, norm_x, norm_mem, w_q, w_k, w_v, w_o, final_norm):
    b_p, t_p, d = x_prompt.shape
    b_s, t_s, _ = x_sample.shape
    depth = norm_mix.shape[0]
    n_mem = mem_prompt.shape[1]
    heads, d_head = cache_mem_k.shape[-2:]
    assert heads == N_HEADS and d_head % LANES == 0 and depth >= 2
    assert d_head & (d_head - 1) == 0 and d_head.bit_length() % 2 == 1, "the score scale must be a power of two"

    nb_p, tt_p, tt_a = b_p, TILE_ROWS // b_p, ATTN_TILE_ROWS // b_p
    nb_s, tt_s = SAMPLE_TILE_ROWS // t_s, t_s
    for nb in (nb_p, nb_s):
        assert nb % SUBLANES == 0 and nb & (nb - 1) == 0
    assert t_p % tt_p == 0 and t_p % tt_a == 0 and b_s % nb_s == 0 and nb_s % XATTN_SEQS == 0

    row = lambda a: a.reshape(a.shape[0], 1, a.shape[-1])
    p = {
        "norm_mix": row(norm_mix), "norm_x": row(norm_x), "norm_mem": row(norm_mem),
        "pool_scale": row(pool_scale), "conv_w": conv_w, "conv_b": row(conv_b), "ln_g": row(ln_g), "ln_b": row(ln_b),
        "short_w": short_w,
    }
    fnorm = final_norm.reshape(1, d)
    w_a, w_b, w_c = pool_scale.shape[-1], conv_b.shape[-1], short_w.shape[-1]

    trunk = {"w_in_even": w_in_even, "pool_w": pool_w, "w_out_even": w_out_even, "w_in_odd": w_in_odd,
             "w_out_odd": w_out_odd, "w_q": w_q, "w_o": w_o}
    kr, vr, kt, vb, *cast = _memory_kv(mem_prompt.reshape(b_p * n_mem, d), p, w_k, w_v,
                                       [w.reshape(-1, w.shape[-1]) for w in trunk.values()], b_p, n_mem, d_head)
    p.update({name: c.reshape(w.shape) for (name, w), c in zip(trunk.items(), cast)})
    mem_k_p = _cache_from_rows(kr, n_mem, heads, d_head)
    mem_v_p = _cache_from_rows(vr, n_mem, heads, d_head)
    kt = kt.reshape(depth, b_p * heads, d_head, n_mem)
    vb = vb.reshape(depth, b_p * heads, n_mem, d_head)

    xp = x_prompt
    zp = jnp.zeros((POOL_HIST * nb_p, w_a), F32)
    zc = jnp.zeros((CONV_HIST * nb_p, w_b), F32)
    zs = jnp.zeros((SHORT_HIST * nb_p, w_c), F32)
    pool_p, conv_p, short_p = [], [], []
    for l in range(depth):
        last = l == depth - 1
        if l % 2 == 0:
            xp, sp, sc = _even_layer(xp, zp, zc, p, l // 2, l, nb=nb_p, tt=tt_p, start_pos=0, carry=True)
            pool_p.append(_from_time_major(sp, b_p, POOL_HIST, nb_p))
            conv_p.append(_from_time_major(sc, b_p, CONV_HIST, nb_p))
        else:
            xp, ss = _odd_layer(xp, zs, p, l // 2, l, nb=nb_p, tt=tt_p, carry=True)
            short_p.append(_from_time_major(ss, b_p, SHORT_HIST, nb_p))
        xp = _xattn_prompt(xp, kt, vb, p, l, nb=nb_p, tt=tt_a, final_norm=fnorm if last else None,
                           batch_major_out=last)
    y_prompt = xp

    krs, vrs = _cache_rows(cache_mem_k), _cache_rows(cache_mem_v)
    xs = _to_time_major(x_sample, nb_s)
    pool_s, conv_s, short_s = [], [], []
    for l in range(depth):
        if l % 2 == 0:
            e = l // 2
            xs, sp, sc = _even_layer(xs, _to_time_major(state_pool[e], nb_s), _to_time_major(state_conv[e], nb_s),
                                     p, e, l, nb=nb_s, tt=tt_s, start_pos=PAST_LEN, carry=False)
            pool_s.append(_from_time_major(sp, b_s, POOL_HIST, nb_s))
            conv_s.append(_from_time_major(sc, b_s, CONV_HIST, nb_s))
        else:
            o = l // 2
            xs, ss = _odd_layer(xs, _to_time_major(state_short[o], nb_s), p, o, l, nb=nb_s, tt=tt_s, carry=False)
            short_s.append(_from_time_major(ss, b_s, SHORT_HIST, nb_s))
        xs = _xattn_sample(xs, krs, vrs, p, l, nbs=nb_s, tm=nb_s * tt_s,
                           final_norm=fnorm if l == depth - 1 else None)
    y_sample = _from_time_major(xs, b_s, t_s, nb_s)

    return (y_prompt, y_sample, jnp.stack(pool_p), jnp.stack(conv_p), jnp.stack(short_p), mem_k_p, mem_v_p,
            jnp.stack(pool_s), jnp.stack(conv_s), jnp.stack(short_s))
```

```python
import functools
import math

import jax
import jax.numpy as jnp
from jax import lax
from jax.experimental import pallas as pl
from jax.experimental.pallas import tpu as pltpu

F32 = jnp.float32
BF16 = jnp.bfloat16

EPS = 1e-6
POOL_WINDOWS = (2, 4, 8, 16)
POOL_HIST = max(POOL_WINDOWS) - 1
CONV_TAPS = 31
CONV_HIST = CONV_TAPS - 1
SHORT_TAPS = 3
SHORT_HIST = SHORT_TAPS - 1
N_HEADS = 4
PAST_LEN = 16384

LANES = 128
SUBLANES = 8
WORD_BYTES = 4
MXU_COLS = 256
VMEM_BYTES = 64 * 1024 * 1024

TILE_ROWS = 512
SAMPLE_TILE_ROWS = 256
ATTN_TILE_ROWS = 1024
NORM_ROWS = 32
CONV_ROWS = 128
CONV_TAP_GROUP = 8
XATTN_SEQS = 8


def _mm(a, b):
    return jnp.dot(a, b, preferred_element_type=F32)


def _silu(x):
    return x * jax.nn.sigmoid(x)


def _rmsnorm(x, g):
    ms = jnp.mean(x * x, axis=-1, keepdims=True)
    return x * lax.rsqrt(ms + EPS) * g


def _softmax(s):
    e = jnp.exp(s - jnp.max(s, axis=-1, keepdims=True))
    return e / jnp.sum(e, axis=-1, keepdims=True)


def _norm_rows(dst_ref, src_ref, g_ref, dtype):
    rows = src_ref.shape[0]
    for r0 in range(0, rows, NORM_ROWS):
        rs = slice(r0, r0 + NORM_ROWS)
        dst_ref[rs, :] = _rmsnorm(src_ref[rs, :], g_ref[...]).astype(dtype)


def _window_rows(hist_ref, new_ref, start, n_rows, cols):
    hist_rows = hist_ref.shape[0]
    n_hist = min(max(hist_rows - start, 0), n_rows)
    parts = []
    if n_hist > 0:
        parts.append(hist_ref[start:start + n_hist, cols])
    if n_hist < n_rows:
        s = start + n_hist - hist_rows
        parts.append(new_ref[s:s + n_rows - n_hist, cols])
    return parts[0] if len(parts) == 1 else jnp.concatenate(parts, axis=0)


def _last_rows(dst_ref, hist_ref, new_ref):
    n, tm = dst_ref.shape[0], new_ref.shape[0]
    if tm >= n:
        dst_ref[...] = new_ref[tm - n:tm, :]
    else:
        dst_ref[0:n - tm, :] = hist_ref[tm:n, :]
        dst_ref[n - tm:n, :] = new_ref[...]


def _exact_zero_of(x):
    bits = lax.bitcast_convert_type(x, jnp.uint32)
    sixteen = jnp.full(bits.shape, 16, jnp.uint32)
    bits = lax.shift_right_logical(lax.shift_right_logical(bits, sixteen), sixteen)
    return lax.bitcast_convert_type(bits, F32)


def _lane_chunks(width):
    return [slice(c0, c0 + LANES) for c0 in range(0, width, LANES)]


def _even_kernel(*refs, nb, tt, start_pos, carry, batch_major_in):
    tm = nb * tt
    it = iter(refs)
    x_ref, hp_ref, hc_ref, nmix_ref, win_ref, poolw_ref, pscale_ref = (next(it) for _ in range(7))
    convw_ref, convb_ref, lng_ref, lnb_ref, wout_ref = (next(it) for _ in range(5))
    o_ref, newp_ref, newc_ref = next(it), next(it), next(it)
    h_scr, pool_scr, vnew_scr, ag_scr, bg_scr, c_scr, ac_scr, wtile_scr = (next(it) for _ in range(8))
    hc_scr = next(it) if carry else None
    if batch_major_in:
        xc_scr, xt_scr = next(it), next(it)

    i = pl.program_id(0)
    n_steps = pl.num_programs(0)
    w_a = pscale_ref.shape[-1]
    w_b = convb_ref.shape[-1]
    hp_rows = POOL_HIST * nb
    pool_group = w_a // len(POOL_WINDOWS)

    if carry:
        @pl.when(i == 0)
        def _():
            pool_scr[0:hp_rows, :] = hp_ref[...]
            hc_scr[...] = hc_ref[...]
        hist_c = hc_scr
    else:
        pool_scr[0:hp_rows, :] = hp_ref[...]
        hist_c = hc_ref

    if batch_major_in:
        for b in range(nb):
            for j, cs in enumerate(_lane_chunks(x_ref.shape[-1])):
                xc_scr[j, pl.ds(b, tt, stride=nb), :] = x_ref[b, :, cs]
        for j, cs in enumerate(_lane_chunks(x_ref.shape[-1])):
            xt_scr[:, cs] = xc_scr[j]
        x_src = xt_scr
    else:
        x_src = x_ref

    _norm_rows(h_scr, x_src, nmix_ref, BF16)

    def column_piece(col0, c0):
        return _mm(h_scr[...], win_ref[:, col0 + c0:col0 + c0 + MXU_COLS])
    for c0 in range(0, w_b, MXU_COLS):
        vnew_scr[:, c0:c0 + MXU_COLS] = column_piece(2 * w_a, c0) * jax.nn.sigmoid(column_piece(2 * w_a + w_b, c0))

    def project(dst_ref, dst_rows, col0, c0):
        def run():
            dst_ref[dst_rows, c0:c0 + MXU_COLS] = column_piece(col0, c0)
        return run
    pieces = ([project(pool_scr, slice(hp_rows, hp_rows + tm), 0, c0) for c0 in range(0, w_a, MXU_COLS)]
              + [project(ag_scr, slice(None), w_a, c0) for c0 in range(0, w_a, MXU_COLS)]
              + [project(bg_scr, slice(None), 2 * w_a + 2 * w_b, c0) for c0 in range(0, w_b, MXU_COLS)])
    conv_chunks = _lane_chunks(w_b)
    for k in range(CONV_TAPS):
        wtile_scr[k * SUBLANES:(k + 1) * SUBLANES, :] = jnp.broadcast_to(convw_ref[k:k + 1, :], (SUBLANES, w_b))

    def conv_chunk(n, cs):
        order = None
        for r0 in range(0, tm, CONV_ROWS):
            bias = jnp.broadcast_to(convb_ref[:, cs], (SUBLANES, LANES))
            if order is not None:
                bias = bias + order
            acc = jnp.concatenate([bias] * (CONV_ROWS // SUBLANES), axis=0)
            for k0 in range(0, CONV_TAPS, CONV_TAP_GROUP):
                taps = range(k0, min(k0 + CONV_TAP_GROUP, CONV_TAPS))
                win = _window_rows(hist_c, vnew_scr, r0 + k0 * nb, (len(taps) - 1) * nb + CONV_ROWS, cs)
                for k in taps:
                    wk = wtile_scr[k * SUBLANES:(k + 1) * SUBLANES, cs]
                    acc = acc + (win[(k - k0) * nb:(k - k0) * nb + CONV_ROWS, :]
                                 * jnp.concatenate([wk] * (CONV_ROWS // SUBLANES), axis=0))
            c_scr[r0:r0 + CONV_ROWS, cs] = acc
            order = _exact_zero_of(acc[0:SUBLANES, :])
        for run in pieces[n * len(pieces) // len(conv_chunks):(n + 1) * len(pieces) // len(conv_chunks)]:
            run()

    for n, cs in enumerate(conv_chunks):
        pl.when(i >= 0)(functools.partial(conv_chunk, n, cs))

    row = lax.broadcasted_iota(jnp.int32, (tm, LANES), 0)
    step = row >> (nb.bit_length() - 1)
    pos1 = step + (start_pos + 1 + (i * tt if carry else 0))
    for g, w in enumerate(POOL_WINDOWS):
        inv_cnt = 1.0 / jnp.minimum(pos1, w).astype(F32)
        a_parts = []
        for c0 in range(g * pool_group, (g + 1) * pool_group, LANES):
            e = pool_scr[(POOL_HIST + 1 - w) * nb:hp_rows + tm, c0:c0 + LANES]
            s, sh = e, 1
            while sh < w:
                n = s.shape[0]
                s = s[sh * nb:, :] + s[:n - sh * nb, :]
                sh *= 2
            a_parts.append((s * inv_cnt - e[(w - 1) * nb:, :]).astype(BF16))
        cs = slice(g * pool_group, (g + 1) * pool_group)
        pa = _mm(jnp.concatenate(a_parts, axis=1), poolw_ref[g])
        ac_scr[:, cs] = (pa * pscale_ref[:, cs] * _silu(ag_scr[:, cs])).astype(BF16)

    o_ref[...] = x_src[...] + _mm(ac_scr[:, 0:w_a], wout_ref[0:w_a, :])

    for r0 in range(0, tm, NORM_ROWS):
        rs = slice(r0, r0 + NORM_ROWS)
        c = c_scr[rs, :]
        d = c - jnp.mean(c, axis=-1, keepdims=True)
        var = jnp.mean(d * d, axis=-1, keepdims=True)
        y = d * lax.rsqrt(var + EPS) * lng_ref[...] + lnb_ref[...]
        ac_scr[rs, w_a:w_a + w_b] = (_silu(y) * _silu(bg_scr[rs, :])).astype(BF16)

    o_ref[...] += _mm(ac_scr[:, w_a:w_a + w_b], wout_ref[w_a:w_a + w_b, :])

    if carry:
        _last_rows(hc_scr, hc_scr, vnew_scr)
        hp_new = pool_scr[tm:tm + hp_rows, :]
        pool_scr[0:hp_rows, :] = hp_new

        @pl.when(i == n_steps - 1)
        def _():
            newp_ref[...] = pool_scr[0:hp_rows, :]
            newc_ref[...] = hc_scr[...]
    else:
        newp_ref[...] = pool_scr[tm:tm + hp_rows, :]
        _last_rows(newc_ref, hc_ref, vnew_scr)


def _odd_kernel(*refs, nb, tt, carry):
    tm = nb * tt
    it = iter(refs)
    x_ref, hs_ref, nmix_ref, win_ref, shortw_ref, wout_ref = (next(it) for _ in range(6))
    o_ref, news_ref = next(it), next(it)
    h_scr, unew_scr, y_scr = (next(it) for _ in range(3))
    hs_scr = next(it) if carry else None

    i = pl.program_id(0)
    n_steps = pl.num_programs(0)
    w_c = shortw_ref.shape[-1]
    d_model = x_ref.shape[-1]

    if carry:
        @pl.when(i == 0)
        def _():
            hs_scr[...] = hs_ref[...]
        hist = hs_scr
    else:
        hist = hs_ref

    _norm_rows(h_scr, x_ref, nmix_ref, BF16)

    def column_piece(col0, c0):
        return _mm(h_scr[...], win_ref[:, col0 + c0:col0 + c0 + MXU_COLS])
    for c0 in range(0, w_c, MXU_COLS):
        unew_scr[:, c0:c0 + MXU_COLS] = column_piece(w_c, c0) * column_piece(2 * w_c, c0)
    for c0 in range(0, w_c, MXU_COLS):
        gate = column_piece(0, c0) * _silu(column_piece(3 * w_c, c0))
        for c1 in range(0, MXU_COLS, LANES):
            cs = slice(c0 + c1, c0 + c1 + LANES)
            acc = _window_rows(hist, unew_scr, 0, tm, cs) * shortw_ref[0:1, cs]
            for k in range(1, SHORT_TAPS):
                acc = acc + _window_rows(hist, unew_scr, k * nb, tm, cs) * shortw_ref[k:k + 1, cs]
            y_scr[:, cs] = (gate[:, c1:c1 + LANES] * acc).astype(BF16)

    o_ref[...] = x_ref[...] + _mm(y_scr[...], wout_ref[...])

    if carry:
        _last_rows(hs_scr, hs_scr, unew_scr)

        @pl.when(i == n_steps - 1)
        def _():
            news_ref[...] = hs_scr[...]
    else:
        _last_rows(news_ref, hs_ref, unew_scr)


def _project_q(h_scr, wq_ref, q_scr, d_head):
    chunks = d_head // LANES
    for hd in range(N_HEADS):
        q = _mm(h_scr[...], wq_ref[:, hd * d_head:(hd + 1) * d_head]) * d_head ** -0.5
        for c in range(chunks):
            q_scr[hd * chunks + c] = q[:, c * LANES:(c + 1) * LANES]


def _pair_slot(seq, j, chunks):
    return seq * N_HEADS + j // chunks, slice((j % chunks) * LANES, (j % chunks + 1) * LANES)


def _xattn_prompt_kernel(*refs, nb, tt, final, batch_major_out):
    it = iter(refs)
    x_ref, nx_ref, wq_ref, kt_ref, v_ref, wo_ref = (next(it) for _ in range(6))
    fn_ref = next(it) if final else None
    o_ref = next(it)
    h_scr, q_scr, oh_scr, qg_scr, y_scr = (next(it) for _ in range(5))
    d_head = v_ref.shape[-1]
    chunks = d_head // LANES
    n_chunks = N_HEADS * chunks

    _norm_rows(h_scr, x_ref, nx_ref, BF16)
    _project_q(h_scr, wq_ref, q_scr, d_head)
    for b in range(nb):
        for j in range(n_chunks):
            pair, cs = _pair_slot(b, j, chunks)
            qg_scr[pair, :, cs] = q_scr[j, pl.ds(b, tt, stride=nb), :].astype(BF16)
    s = jnp.einsum("pqd,pdk->pqk", qg_scr[...], kt_ref[...], preferred_element_type=F32)
    o = jnp.einsum("pqk,pkd->pqd", _softmax(s).astype(BF16), v_ref[...], preferred_element_type=F32)
    for b in range(nb):
        for j in range(n_chunks):
            pair, cs = _pair_slot(b, j, chunks)
            oh_scr[j, pl.ds(b, tt, stride=nb), :] = o[pair, :, cs]
    o_all = jnp.concatenate([oh_scr[j] for j in range(n_chunks)], axis=1)
    out = x_ref[...] + _mm(o_all.astype(BF16), wo_ref[...])

    if not (final or batch_major_out):
        o_ref[...] = out
        return
    y_scr[...] = out
    if final:
        _norm_rows(y_scr, y_scr, fn_ref, F32)
    if not batch_major_out:
        o_ref[...] = y_scr[...]
        return
    for j, cs in enumerate(_lane_chunks(y_scr.shape[-1])):
        q_scr[j] = y_scr[:, cs]
    for b in range(nb):
        for j, cs in enumerate(_lane_chunks(y_scr.shape[-1])):
            o_ref[b, :, cs] = q_scr[j, pl.ds(b, tt, stride=nb), :]


def _xattn_sample_kernel(*refs, nbs, final):
    it = iter(refs)
    x_ref, nx_ref, wq_ref, k_ref, v_ref, wo_ref = (next(it) for _ in range(6))
    fn_ref = next(it) if final else None
    o_ref = next(it)
    h_scr, q_scr, oh_scr, y_scr, qg_scr, kg_scr, vg_scr = (next(it) for _ in range(7))

    c = pl.program_id(1)
    tt = x_ref.shape[0] // nbs
    seqs = k_ref.shape[0]
    d_head = wq_ref.shape[-1] // N_HEADS
    chunks = d_head // LANES
    n_chunks = N_HEADS * chunks
    n_mem = k_ref.shape[1] // n_chunks

    @pl.when(c == 0)
    def _():
        _norm_rows(h_scr, x_ref, nx_ref, BF16)
        _project_q(h_scr, wq_ref, q_scr, d_head)

    for bl in range(seqs):
        rows = pl.ds(c * seqs + bl, tt, stride=nbs)
        for j in range(n_chunks):
            pair, cs = _pair_slot(bl, j, chunks)
            cache_rows = pl.ds((j % chunks) * N_HEADS + j // chunks, n_mem, stride=n_chunks)
            qg_scr[pair, :, cs] = q_scr[j, rows, :]
            kg_scr[pair, :, cs] = k_ref[bl, cache_rows, :].astype(BF16)
            vg_scr[pair, :, cs] = v_ref[bl, cache_rows, :].astype(BF16)
    s = jnp.einsum("pqd,pkd->pqk", qg_scr[...].astype(BF16), kg_scr[...], preferred_element_type=F32)
    o = jnp.einsum("pqk,pkd->pqd", _softmax(s).astype(BF16), vg_scr[...], preferred_element_type=F32)
    for bl in range(seqs):
        rows = pl.ds(c * seqs + bl, tt, stride=nbs)
        for j in range(n_chunks):
            pair, cs = _pair_slot(bl, j, chunks)
            oh_scr[j, rows, :] = o[pair, :, cs]

    @pl.when(c == pl.num_programs(1) - 1)
    def _():
        o_all = jnp.concatenate([oh_scr[j] for j in range(n_chunks)], axis=1)
        out = x_ref[...] + _mm(o_all.astype(BF16), wo_ref[...])
        if final:
            y_scr[...] = out
            _norm_rows(o_ref, y_scr, fn_ref, F32)
        else:
            o_ref[...] = out


def _memory_kv_kernel(*refs, n_casts):
    mem_ref, nmem_ref, wk_ref, wv_ref = refs[:4]
    cast_in = refs[4:4 + n_casts]
    kr_ref, vr_ref, kt_ref, vb_ref = refs[4 + n_casts:8 + n_casts]
    cast_out = refs[8 + n_casts:8 + 2 * n_casts]
    h_scr, wkb_scr, wvb_scr = refs[8 + 2 * n_casts:]
    for src, dst in zip(cast_in, cast_out):
        dst[...] = src[...].astype(BF16)

    @pl.when(pl.program_id(1) == 0)
    def _():
        wkb_scr[...] = wk_ref[...].astype(BF16)
        wvb_scr[...] = wv_ref[...].astype(BF16)

    n_mem = mem_ref.shape[0]
    d_head = vb_ref.shape[-1]
    chunks = d_head // LANES
    n_chunks = chunks * N_HEADS
    _norm_rows(h_scr, mem_ref, nmem_ref, BF16)
    for w_ref, r_ref in ((wkb_scr, kr_ref), (wvb_scr, vr_ref)):
        for hd in range(N_HEADS):
            kh = _mm(h_scr[...], w_ref[:, hd * d_head:(hd + 1) * d_head])
            if r_ref is kr_ref:
                kt_ref[hd] = kh.T.astype(BF16)
            else:
                vb_ref[hd] = kh.astype(BF16)
            for cc in range(chunks):
                r_ref[pl.ds(cc * N_HEADS + hd, n_mem, stride=n_chunks), :] = kh[:, cc * LANES:(cc + 1) * LANES]


def _const_spec(shape, layer=None):
    if layer is None:
        return pl.BlockSpec(shape, lambda *_: (0,) * len(shape), pipeline_mode=pl.Buffered(1))
    return pl.BlockSpec((None,) + shape, lambda *_: (layer,) + (0,) * len(shape), pipeline_mode=pl.Buffered(1))


def _row_spec(rows, cols):
    return pl.BlockSpec((rows, cols), lambda i: (i, 0))


def _tile_rows(dtype):
    return SUBLANES * WORD_BYTES // jnp.dtype(dtype).itemsize


def _buffer_bytes(shape, dtype):
    itemsize = jnp.dtype(dtype).itemsize
    dims = [s for s in shape if s is not None]
    rows = _tile_rows(dtype)
    if len(dims) >= 2:
        dims[-2] = -(-dims[-2] // rows) * rows
    dims[-1] = -(-dims[-1] // LANES) * LANES
    return math.prod(dims) * itemsize


def _pallas(body, *, grid, args, in_specs, out_specs, out_shape, scratch, value_bytes, name):
    single = not isinstance(out_specs, (list, tuple))
    o_specs, o_shapes = ([out_specs], [out_shape]) if single else (out_specs, out_shape)
    buffers = lambda spec: spec.pipeline_mode.buffer_count if spec.pipeline_mode is not None else 2
    limit = (sum(_buffer_bytes(s.block_shape, a.dtype) * buffers(s) for s, a in zip(in_specs, args))
             + sum(_buffer_bytes(s.block_shape, o.dtype) * buffers(s) for s, o in zip(o_specs, o_shapes))
             + sum(_buffer_bytes(m.shape, m.dtype) for m in scratch) + value_bytes)
    assert limit <= VMEM_BYTES, (name, limit)
    return pl.pallas_call(
        body, grid=grid, in_specs=in_specs, out_specs=out_specs, out_shape=out_shape, scratch_shapes=scratch,
        compiler_params=pltpu.CompilerParams(dimension_semantics=("arbitrary",) * len(grid), vmem_limit_bytes=limit),
        name=name)(*args)


def _chunk_major(tm, d):
    return pltpu.VMEM((d // LANES, tm, LANES), F32)


def _even_layer(x, hp, hc, p, e, l, *, nb, tt, start_pos, carry):
    batch_major_in = x.ndim == 3
    d = x.shape[-1]
    tm = nb * tt
    rows = x.shape[0] * x.shape[1] if batch_major_in else x.shape[0]
    w_a, w_b = p["pool_scale"].shape[-1], p["conv_b"].shape[-1]
    hp_rows, hc_rows = POOL_HIST * nb, CONV_HIST * nb
    assert not carry or tm >= hc_rows
    assert not batch_major_in or (carry and x.shape[0] == nb)
    n_groups = len(POOL_WINDOWS)

    args = [x, hp, hc, p["norm_mix"], p["w_in_even"], p["pool_w"], p["pool_scale"], p["conv_w"], p["conv_b"],
            p["ln_g"], p["ln_b"], p["w_out_even"]]
    hist_spec = (lambda r, c: _const_spec((r, c))) if carry else _row_spec
    x_spec = pl.BlockSpec((nb, tt, d), lambda i: (0, i, 0)) if batch_major_in else _row_spec(tm, d)
    specs = [x_spec, hist_spec(hp_rows, w_a), hist_spec(hc_rows, w_b),
             _const_spec((1, d), l), _const_spec((d, 2 * w_a + 3 * w_b), e),
             _const_spec((n_groups, w_a // n_groups, w_a // n_groups), e), _const_spec((1, w_a), e),
             _const_spec((CONV_TAPS, w_b), e), _const_spec((1, w_b), e), _const_spec((1, w_b), e),
             _const_spec((1, w_b), e), _const_spec((w_a + w_b, d), e)]

    state_spec = (lambda r, c: pl.BlockSpec((r, c), lambda i: (0, 0))) if carry else _row_spec
    n_state = 1 if carry else rows // tm
    scratch = [pltpu.VMEM((tm, d), BF16), pltpu.VMEM((hp_rows + tm, w_a), F32), pltpu.VMEM((tm, w_b), F32),
               pltpu.VMEM((tm, w_a), F32), pltpu.VMEM((tm, w_b), F32), pltpu.VMEM((tm, w_b), F32),
               pltpu.VMEM((tm, w_a + w_b), BF16), pltpu.VMEM((CONV_TAPS * SUBLANES, w_b), F32)]
    if carry:
        scratch.append(pltpu.VMEM((hc_rows, w_b), F32))
    if batch_major_in:
        scratch += [_chunk_major(tm, d), pltpu.VMEM((tm, d), F32)]
    return _pallas(
        functools.partial(_even_kernel, nb=nb, tt=tt, start_pos=start_pos, carry=carry,
                          batch_major_in=batch_major_in),
        grid=(rows // tm,), args=args, in_specs=specs,
        out_specs=[_row_spec(tm, d), state_spec(hp_rows, w_a), state_spec(hc_rows, w_b)],
        out_shape=[jax.ShapeDtypeStruct((rows, d), F32), jax.ShapeDtypeStruct((n_state * hp_rows, w_a), F32),
                   jax.ShapeDtypeStruct((n_state * hc_rows, w_b), F32)],
        scratch=scratch, value_bytes=2 * _buffer_bytes((tm, d), F32),
        name=f"even_mixer_{l}_{'carry' if carry else 'group'}")


def _odd_layer(x, hs, p, o, l, *, nb, tt, carry):
    rows, d = x.shape
    tm = nb * tt
    w_c = p["short_w"].shape[-1]
    hs_rows = SHORT_HIST * nb
    assert not carry or tm >= hs_rows

    args = [x, hs, p["norm_mix"], p["w_in_odd"], p["short_w"], p["w_out_odd"]]
    hist_spec = (lambda r, c: _const_spec((r, c))) if carry else _row_spec
    specs = [_row_spec(tm, d), hist_spec(hs_rows, w_c), _const_spec((1, d), l), _const_spec((d, 4 * w_c), o),
             _const_spec((SHORT_TAPS, w_c), o), _const_spec((w_c, d), o)]
    state_spec = (lambda r, c: pl.BlockSpec((r, c), lambda i: (0, 0))) if carry else _row_spec
    n_state = 1 if carry else rows // tm
    scratch = [pltpu.VMEM((tm, d), BF16), pltpu.VMEM((tm, w_c), F32), pltpu.VMEM((tm, w_c), BF16)]
    if carry:
        scratch.append(pltpu.VMEM((hs_rows, w_c), F32))
    return _pallas(
        functools.partial(_odd_kernel, nb=nb, tt=tt, carry=carry),
        grid=(rows // tm,), args=args, in_specs=specs,
        out_specs=[_row_spec(tm, d), state_spec(hs_rows, w_c)],
        out_shape=[jax.ShapeDtypeStruct((rows, d), F32), jax.ShapeDtypeStruct((n_state * hs_rows, w_c), F32)],
        scratch=scratch, value_bytes=2 * _buffer_bytes((tm, d), F32),
        name=f"odd_mixer_{l}_{'carry' if carry else 'group'}")


def _xattn_prompt(x, kt, vb, p, l, *, nb, tt, final_norm=None, batch_major_out=False):
    rows, d = x.shape
    tm = nb * tt
    n_pairs, d_head = nb * N_HEADS, d // N_HEADS
    assert kt.shape[1] == n_pairs and vb.shape[1] == n_pairs
    final = final_norm is not None
    args = [x, p["norm_x"], p["w_q"], kt, vb, p["w_o"]]
    specs = [_row_spec(tm, d), _const_spec((1, d), l), _const_spec((d, d), l), _const_spec(kt.shape[1:], l),
             _const_spec(vb.shape[1:], l), _const_spec((d, d), l)]
    if final:
        args, specs = args + [final_norm], specs + [_const_spec((1, d))]
    if batch_major_out:
        out_spec = pl.BlockSpec((nb, tt, d), lambda i: (0, i, 0))
        out_shape = jax.ShapeDtypeStruct((nb, rows // nb, d), F32)
    else:
        out_spec, out_shape = _row_spec(tm, d), jax.ShapeDtypeStruct((rows, d), F32)
    return _pallas(
        functools.partial(_xattn_prompt_kernel, nb=nb, tt=tt, final=final, batch_major_out=batch_major_out),
        grid=(rows // tm,), args=args, in_specs=specs, out_specs=out_spec, out_shape=out_shape,
        scratch=[pltpu.VMEM((tm, d), BF16), _chunk_major(tm, d), _chunk_major(tm, d),
                 pltpu.VMEM((n_pairs, tt, d_head), BF16), pltpu.VMEM((tm, d), F32)],
        value_bytes=4 * _buffer_bytes((tm, d), F32),
        name=f"xattn_prompt_{l}")


def _xattn_sample(x, kr, vr, p, l, *, nbs, tm, final_norm=None):
    rows, d = x.shape
    n_chunks = nbs // XATTN_SEQS
    final = final_norm is not None
    d_head = d // N_HEADS
    n_pairs = XATTN_SEQS * N_HEADS
    n_mem = kr.shape[2] * LANES // d
    cache_block = (XATTN_SEQS,) + kr.shape[2:]
    cache_spec = pl.BlockSpec((None,) + cache_block, lambda g, c: (l, g * n_chunks + c, 0, 0))
    args = [x, p["norm_x"], p["w_q"], kr, vr, p["w_o"]]
    specs = [pl.BlockSpec((tm, d), lambda g, c: (g, 0)), _const_spec((1, d), l), _const_spec((d, d), l),
             cache_spec, cache_spec, _const_spec((d, d), l)]
    if final:
        args, specs = args + [final_norm], specs + [_const_spec((1, d))]
    return _pallas(
        functools.partial(_xattn_sample_kernel, nbs=nbs, final=final),
        grid=(rows // tm, n_chunks), args=args, in_specs=specs,
        out_specs=pl.BlockSpec((tm, d), lambda g, c: (g, 0)), out_shape=jax.ShapeDtypeStruct((rows, d), F32),
        scratch=[pltpu.VMEM((tm, d), BF16), _chunk_major(tm, d), _chunk_major(tm, d), pltpu.VMEM((tm, d), F32),
                 pltpu.VMEM((n_pairs, tm // nbs, d_head), F32), pltpu.VMEM((n_pairs, n_mem, d_head), BF16),
                 pltpu.VMEM((n_pairs, n_mem, d_head), BF16)],
        value_bytes=4 * _buffer_bytes((tm, d), F32),
        name=f"xattn_sample_{l}")


def _memory_kv(mem, p, w_k, w_v, casts, n_seq, n_mem, d_head):
    d = mem.shape[-1]
    depth = w_k.shape[0]
    n_steps = depth * n_seq
    cast_specs = []
    for w in casts:
        assert w.shape[0] % (n_steps * _tile_rows(BF16)) == 0
        cast_specs.append(pl.BlockSpec((w.shape[0] // n_steps, w.shape[1]), lambda l, b: (l * n_seq + b, 0)))
    r_rows = n_mem * d // LANES
    r_spec = pl.BlockSpec((None, None, r_rows, LANES), lambda l, b: (l, b, 0, 0))
    kt_spec = pl.BlockSpec((None, None, N_HEADS, d_head, n_mem), lambda l, b: (l, b, 0, 0, 0))
    vb_spec = pl.BlockSpec((None, None, N_HEADS, n_mem, d_head), lambda l, b: (l, b, 0, 0, 0))
    w_spec = pl.BlockSpec((None, d, d), lambda l, b: (l, 0, 0))
    r_shape = jax.ShapeDtypeStruct((depth, n_seq, r_rows, LANES), F32)
    return _pallas(
        functools.partial(_memory_kv_kernel, n_casts=len(casts)),
        grid=(depth, n_seq), args=[mem, p["norm_mem"], w_k, w_v, *casts],
        in_specs=[pl.BlockSpec((n_mem, d), lambda l, b: (b, 0)), pl.BlockSpec((None, 1, d), lambda l, b: (l, 0, 0)),
                  w_spec, w_spec] + cast_specs,
        out_specs=[r_spec, r_spec, kt_spec, vb_spec] + cast_specs,
        out_shape=[r_shape, r_shape, jax.ShapeDtypeStruct((depth, n_seq, N_HEADS, d_head, n_mem), BF16),
                   jax.ShapeDtypeStruct((depth, n_seq, N_HEADS, n_mem, d_head), BF16)]
        + [jax.ShapeDtypeStruct(w.shape, BF16) for w in casts],
        scratch=[pltpu.VMEM((n_mem, d), BF16), pltpu.VMEM((d, d), BF16), pltpu.VMEM((d, d), BF16)],
        value_bytes=2 * _buffer_bytes((n_mem, d), F32),
        name="memory_kv")


def _to_time_major(x, nb):
    b, t, c = x.shape
    return x.reshape(b // nb, nb, t, c).transpose(0, 2, 1, 3).reshape(b * t, c)


def _from_time_major(y, b, t, nb):
    c = y.shape[-1]
    return y.reshape(b // nb, t, nb, c).transpose(0, 2, 1, 3).reshape(b, t, c)


def _cache_rows(c):
    *lead, n_mem, heads, d_head = c.shape
    chunks = d_head // LANES
    r = c.reshape(*lead, n_mem, heads, chunks, LANES)
    return jnp.swapaxes(r, -3, -2).reshape(*lead, n_mem * chunks * heads, LANES)


def _cache_from_rows(r, n_mem, heads, d_head):
    *lead, _, _ = r.shape
    chunks = d_head // LANES
    c = r.reshape(*lead, n_mem, chunks, heads, LANES)
    return jnp.swapaxes(c, -3, -2).reshape(*lead, n_mem, heads, d_head)


def kernel(x_prompt, x_sample, state_pool, state_conv, state_short, cache_mem_k, cache_mem_v, mem_prompt, norm_mix, w_in_even, pool_w, pool_scale, conv_w, conv_b, ln_g, ln_b, w_out_even, w_in_odd, short_w, w_out_odd, norm_x, norm_mem, w_q, w_k, w_v, w_o, final_norm):
    b_p, t_p, d = x_prompt.shape
    b_s, t_s, _ = x_sample.shape
    depth = norm_mix.shape[0]
    n_mem = mem_prompt.shape[1]
    heads, d_head = cache_mem_k.shape[-2:]
    assert heads == N_HEADS and d_head % LANES == 0 and depth >= 2
    assert d_head & (d_head - 1) == 0 and d_head.bit_length() % 2 == 1, "the score scale must be a power of two"

    nb_p, tt_p, tt_a = b_p, TILE_ROWS // b_p, ATTN_TILE_ROWS // b_p
    nb_s, tt_s = SAMPLE_TILE_ROWS // t_s, t_s
    for nb in (nb_p, nb_s):
        assert nb % SUBLANES == 0 and nb & (nb - 1) == 0
    assert t_p % tt_p == 0 and t_p % tt_a == 0 and b_s % nb_s == 0 and nb_s % XATTN_SEQS == 0

    row = lambda a: a.reshape(a.shape[0], 1, a.shape[-1])
    p = {
        "norm_mix": row(norm_mix), "norm_x": row(norm_x), "norm_mem": row(norm_mem),
        "pool_scale": row(pool_scale), "conv_w": conv_w, "conv_b": row(conv_b), "ln_g": row(ln_g), "ln_b": row(ln_b),
        "short_w": short_w,
    }
    fnorm = final_norm.reshape(1, d)
    w_a, w_b, w_c = pool_scale.shape[-1], conv_b.shape[-1], short_w.shape[-1]

    trunk = {"w_in_even": w_in_even, "pool_w": pool_w, "w_out_even": w_out_even, "w_in_odd": w_in_odd,
             "w_out_odd": w_out_odd, "w_q": w_q, "w_o": w_o}
    kr, vr, kt, vb, *cast = _memory_kv(mem_prompt.reshape(b_p * n_mem, d), p, w_k, w_v,
                                       [w.reshape(-1, w.shape[-1]) for w in trunk.values()], b_p, n_mem, d_head)
    p.update({name: c.reshape(w.shape) for (name, w), c in zip(trunk.items(), cast)})
    mem_k_p = _cache_from_rows(kr, n_mem, heads, d_head)
    mem_v_p = _cache_from_rows(vr, n_mem, heads, d_head)
    kt = kt.reshape(depth, b_p * heads, d_head, n_mem)
    vb = vb.reshape(depth, b_p * heads, n_mem, d_head)

    xp = x_prompt
    zp = jnp.zeros((POOL_HIST * nb_p, w_a), F32)
    zc = jnp.zeros((CONV_HIST * nb_p, w_b), F32)
    zs = jnp.zeros((SHORT_HIST * nb_p, w_c), F32)
    pool_p, conv_p, short_p = [], [], []
    for l in range(depth):
        last = l == depth - 1
        if l % 2 == 0:
            xp, sp, sc = _even_layer(xp, zp, zc, p, l // 2, l, nb=nb_p, tt=tt_p, start_pos=0, carry=True)
            pool_p.append(_from_time_major(sp, b_p, POOL_HIST, nb_p))
            conv_p.append(_from_time_major(sc, b_p, CONV_HIST, nb_p))
        else:
            xp, ss = _odd_layer(xp, zs, p, l // 2, l, nb=nb_p, tt=tt_p, carry=True)
            short_p.append(_from_time_major(ss, b_p, SHORT_HIST, nb_p))
        xp = _xattn_prompt(xp, kt, vb, p, l, nb=nb_p, tt=tt_a, final_norm=fnorm if last else None,
                           batch_major_out=last)
    y_prompt = xp

    krs, vrs = _cache_rows(cache_mem_k), _cache_rows(cache_mem_v)
    xs = _to_time_major(x_sample, nb_s)
    pool_s, conv_s, short_s = [], [], []
    for l in range(depth):
        if l % 2 == 0:
            e = l // 2
            xs, sp, sc = _even_layer(xs, _to_time_major(state_pool[e], nb_s), _to_time_major(state_conv[e], nb_s),
                                     p, e, l, nb=nb_s, tt=tt_s, start_pos=PAST_LEN, carry=False)
            pool_s.append(_from_time_major(sp, b_s, POOL_HIST, nb_s))
            conv_s.append(_from_time_major(sc, b_s, CONV_HIST, nb_s))
        else:
            o = l // 2
            xs, ss = _odd_layer(xs, _to_time_major(state_short[o], nb_s), p, o, l, nb=nb_s, tt=tt_s, carry=False)
            short_s.append(_from_time_major(ss, b_s, SHORT_HIST, nb_s))
        xs = _xattn_sample(xs, krs, vrs, p, l, nbs=nb_s, tm=nb_s * tt_s,
                           final_norm=fnorm if l == depth - 1 else None)
    y_sample = _from_time_major(xs, b_s, t_s, nb_s)

    return (y_prompt, y_sample, jnp.stack(pool_p), jnp.stack(conv_p), jnp.stack(short_p), mem_k_p, mem_v_p,
            jnp.stack(pool_s), jnp.stack(conv_s), jnp.stack(short_s))
```

```python
import functools
import math

import jax
import jax.numpy as jnp
from jax import lax
from jax.experimental import pallas as pl
from jax.experimental.pallas import tpu as pltpu

F32 = jnp.float32
BF16 = jnp.bfloat16

EPS = 1e-6
POOL_WINDOWS = (2, 4, 8, 16)
POOL_HIST = max(POOL_WINDOWS) - 1
CONV_TAPS = 31
CONV_HIST = CONV_TAPS - 1
SHORT_TAPS = 3
SHORT_HIST = SHORT_TAPS - 1
N_HEADS = 4
PAST_LEN = 16384

LANES = 128
SUBLANES = 8
WORD_BYTES = 4
MXU_COLS = 256
VMEM_BYTES = 64 * 1024 * 1024

TILE_ROWS = 512
SAMPLE_TILE_ROWS = 256
ATTN_TILE_ROWS = 1024
NORM_ROWS = 32
CONV_ROWS = 128
CONV_TAP_GROUP = 8
XATTN_SEQS = 8


def _mm(a, b):
    return jnp.dot(a, b, preferred_element_type=F32)


def _silu(x):
    return x * jax.nn.sigmoid(x)


def _rmsnorm(x, g):
    ms = jnp.mean(x * x, axis=-1, keepdims=True)
    return x * lax.rsqrt(ms + EPS) * g


def _softmax(s):
    e = jnp.exp(s - jnp.max(s, axis=-1, keepdims=True))
    return e / jnp.sum(e, axis=-1, keepdims=True)


def _norm_rows(dst_ref, src_ref, g_ref, dtype):
    rows = src_ref.shape[0]
    for r0 in range(0, rows, NORM_ROWS):
        rs = slice(r0, r0 + NORM_ROWS)
        dst_ref[rs, :] = _rmsnorm(src_ref[rs, :], g_ref[...]).astype(dtype)


def _window_rows(hist_ref, new_ref, start, n_rows, cols):
    hist_rows = hist_ref.shape[0]
    n_hist = min(max(hist_rows - start, 0), n_rows)
    parts = []
    if n_hist > 0:
        parts.append(hist_ref[start:start + n_hist, cols])
    if n_hist < n_rows:
        s = start + n_hist - hist_rows
        parts.append(new_ref[s:s + n_rows - n_hist, cols])
    return parts[0] if len(parts) == 1 else jnp.concatenate(parts, axis=0)


def _last_rows(dst_ref, hist_ref, new_ref):
    n, tm = dst_ref.shape[0], new_ref.shape[0]
    if tm >= n:
        dst_ref[...] = new_ref[tm - n:tm, :]
    else:
        dst_ref[0:n - tm, :] = hist_ref[tm:n, :]
        dst_ref[n - tm:n, :] = new_ref[...]


def _exact_zero_of(x):
    bits = lax.bitcast_convert_type(x, jnp.uint32)
    sixteen = jnp.full(bits.shape, 16, jnp.uint32)
    bits = lax.shift_right_logical(lax.shift_right_logical(bits, sixteen), sixteen)
    return lax.bitcast_convert_type(bits, F32)


def _lane_chunks(width):
    return [slice(c0, c0 + LANES) for c0 in range(0, width, LANES)]


def _even_kernel(*refs, nb, tt, start_pos, carry, batch_major_in):
    tm = nb * tt
    it = iter(refs)
    x_ref, hp_ref, hc_ref, nmix_ref, win_ref, poolw_ref, pscale_ref = (next(it) for _ in range(7))
    convw_ref, convb_ref, lng_ref, lnb_ref, wout_ref = (next(it) for _ in range(5))
    o_ref, newp_ref, newc_ref = next(it), next(it), next(it)
    h_scr, pool_scr, vnew_scr, ag_scr, bg_scr, c_scr, ac_scr, wtile_scr = (next(it) for _ in range(8))
    hc_scr = next(it) if carry else None
    if batch_major_in:
        xc_scr, xt_scr = next(it), next(it)

    i = pl.program_id(0)
    n_steps = pl.num_programs(0)
    w_a = pscale_ref.shape[-1]
    w_b = convb_ref.shape[-1]
    hp_rows = POOL_HIST * nb
    pool_group = w_a // len(POOL_WINDOWS)

    if carry:
        @pl.when(i == 0)
        def _():
            pool_scr[0:hp_rows, :] = hp_ref[...]
            hc_scr[...] = hc_ref[...]
        hist_c = hc_scr
    else:
        pool_scr[0:hp_rows, :] = hp_ref[...]
        hist_c = hc_ref

    if batch_major_in:
        for b in range(nb):
            for j, cs in enumerate(_lane_chunks(x_ref.shape[-1])):
                xc_scr[j, pl.ds(b, tt, stride=nb), :] = x_ref[b, :, cs]
        for j, cs in enumerate(_lane_chunks(x_ref.shape[-1])):
            xt_scr[:, cs] = xc_scr[j]
        x_src = xt_scr
    else:
        x_src = x_ref

    _norm_rows(h_scr, x_src, nmix_ref, BF16)

    def column_piece(col0, c0):
        return _mm(h_scr[...], win_ref[:, col0 + c0:col0 + c0 + MXU_COLS])
    for c0 in range(0, w_b, MXU_COLS):
        vnew_scr[:, c0:c0 + MXU_COLS] = column_piece(2 * w_a, c0) * jax.nn.sigmoid(column_piece(2 * w_a + w_b, c0))

    def project(dst_ref, dst_rows, col0, c0):
        def run():
            dst_ref[dst_rows, c0:c0 + MXU_COLS] = column_piece(col0, c0)
        return run
    pieces = ([project(pool_scr, slice(hp_rows, hp_rows + tm), 0, c0) for c0 in range(0, w_a, MXU_COLS)]
              + [project(ag_scr, slice(None), w_a, c0) for c0 in range(0, w_a, MXU_COLS)]
              + [project(bg_scr, slice(None), 2 * w_a + 2 * w_b, c0) for c0 in range(0, w_b, MXU_COLS)])
    conv_chunks = _lane_chunks(w_b)
    for k in range(CONV_TAPS):
        wtile_scr[k * SUBLANES:(k + 1) * SUBLANES, :] = jnp.broadcast_to(convw_ref[k:k + 1, :], (SUBLANES, w_b))

    def conv_chunk(n, cs):
        order = None
        for r0 in range(0, tm, CONV_ROWS):
            bias = jnp.broadcast_to(convb_ref[:, cs], (SUBLANES, LANES))
            if order is not None:
                bias = bias + order
            acc = jnp.concatenate([bias] * (CONV_ROWS // SUBLANES), axis=0)
            for k0 in range(0, CONV_TAPS, CONV_TAP_GROUP):
                taps = range(k0, min(k0 + CONV_TAP_GROUP, CONV_TAPS))
                win = _window_rows(hist_c, vnew_scr, r0 + k0 * nb, (len(taps) - 1) * nb + CONV_ROWS, cs)
                for k in taps:
                    wk = wtile_scr[k * SUBLANES:(k + 1) * SUBLANES, cs]
                    acc = acc + (win[(k - k0) * nb:(k - k0) * nb + CONV_ROWS, :]
                                 * jnp.concatenate([wk] * (CONV_ROWS // SUBLANES), axis=0))
            c_scr[r0:r0 + CONV_ROWS, cs] = acc
            order = _exact_zero_of(acc[0:SUBLANES, :])
        for run in pieces[n * len(pieces) // len(conv_chunks):(n + 1) * len(pieces) // len(conv_chunks)]:
            run()

    for n, cs in enumerate(conv_chunks):
        pl.when(i >= 0)(functools.partial(conv_chunk, n, cs))

    row = lax.broadcasted_iota(jnp.int32, (tm, LANES), 0)
    step = row >> (nb.bit_length() - 1)
    pos1 = step + (start_pos + 1 + (i * tt if carry else 0))
    for g, w in enumerate(POOL_WINDOWS):
        inv_cnt = 1.0 / jnp.minimum(pos1, w).astype(F32)
        a_parts = []
        for c0 in range(g * pool_group, (g + 1) * pool_group, LANES):
            e = pool_scr[(POOL_HIST + 1 - w) * nb:hp_rows + tm, c0:c0 + LANES]
            s, sh = e, 1
            while sh < w:
                n = s.shape[0]
                s = s[sh * nb:, :] + s[:n - sh * nb, :]
                sh *= 2
            a_parts.append((s * inv_cnt - e[(w - 1) * nb:, :]).astype(BF16))
        cs = slice(g * pool_group, (g + 1) * pool_group)
        pa = _mm(jnp.concatenate(a_parts, axis=1), poolw_ref[g])
        ac_scr[:, cs] = (pa * pscale_ref[:, cs] * _silu(ag_scr[:, cs])).astype(BF16)

    o_ref[...] = x_src[...] + _mm(ac_scr[:, 0:w_a], wout_ref[0:w_a, :])

    for r0 in range(0, tm, NORM_ROWS):
        rs = slice(r0, r0 + NORM_ROWS)
        c = c_scr[rs, :]
        d = c - jnp.mean(c, axis=-1, keepdims=True)
        var = jnp.mean(d * d, axis=-1, keepdims=True)
        y = d * lax.rsqrt(var + EPS) * lng_ref[...] + lnb_ref[...]
        ac_scr[rs, w_a:w_a + w_b] = (_silu(y) * _silu(bg_scr[rs, :])).astype(BF16)

    o_ref[...] += _mm(ac_scr[:, w_a:w_a + w_b], wout_ref[w_a:w_a + w_b, :])

    if carry:
        _last_rows(hc_scr, hc_scr, vnew_scr)
        hp_new = pool_scr[tm:tm + hp_rows, :]
        pool_scr[0:hp_rows, :] = hp_new

        @pl.when(i == n_steps - 1)
        def _():
            newp_ref[...] = pool_scr[0:hp_rows, :]
            newc_ref[...] = hc_scr[...]
    else:
        newp_ref[...] = pool_scr[tm:tm + hp_rows, :]
        _last_rows(newc_ref, hc_ref, vnew_scr)


def _odd_kernel(*refs, nb, tt, carry):
    tm = nb * tt
    it = iter(refs)
    x_ref, hs_ref, nmix_ref, win_ref, shortw_ref, wout_ref = (next(it) for _ in range(6))
    o_ref, news_ref = next(it), next(it)
    h_scr, unew_scr, y_scr = (next(it) for _ in range(3))
    hs_scr = next(it) if carry else None

    i = pl.program_id(0)
    n_steps = pl.num_programs(0)
    w_c = shortw_ref.shape[-1]
    d_model = x_ref.shape[-1]

    if carry:
        @pl.when(i == 0)
        def _():
            hs_scr[...] = hs_ref[...]
        hist = hs_scr
    else:
        hist = hs_ref

    _norm_rows(h_scr, x_ref, nmix_ref, BF16)

    def column_piece(col0, c0):
        return _mm(h_scr[...], win_ref[:, col0 + c0:col0 + c0 + MXU_COLS])
    for c0 in range(0, w_c, MXU_COLS):
        unew_scr[:, c0:c0 + MXU_COLS] = column_piece(w_c, c0) * column_piece(2 * w_c, c0)
    for c0 in range(0, w_c, MXU_COLS):
        gate = column_piece(0, c0) * _silu(column_piece(3 * w_c, c0))
        for c1 in range(0, MXU_COLS, LANES):
            cs = slice(c0 + c1, c0 + c1 + LANES)
            acc = _window_rows(hist, unew_scr, 0, tm, cs) * shortw_ref[0:1, cs]
            for k in range(1, SHORT_TAPS):
                acc = acc + _window_rows(hist, unew_scr, k * nb, tm, cs) * shortw_ref[k:k + 1, cs]
            y_scr[:, cs] = (gate[:, c1:c1 + LANES] * acc).astype(BF16)

    o_ref[...] = x_ref[...] + _mm(y_scr[...], wout_ref[...])

    if carry:
        _last_rows(hs_scr, hs_scr, unew_scr)

        @pl.when(i == n_steps - 1)
        def _():
            news_ref[...] = hs_scr[...]
    else:
        _last_rows(news_ref, hs_ref, unew_scr)


def _project_q(h_scr, wq_ref, q_scr, d_head):
    chunks = d_head // LANES
    for hd in range(N_HEADS):
        q = _mm(h_scr[...], wq_ref[:, hd * d_head:(hd + 1) * d_head]) * d_head ** -0.5
        for c in range(chunks):
            q_scr[hd * chunks + c] = q[:, c * LANES:(c + 1) * LANES]


def _pair_slot(seq, j, chunks):
    return seq * N_HEADS + j // chunks, slice((j % chunks) * LANES, (j % chunks + 1) * LANES)


def _xattn_prompt_kernel(*refs, nb, tt, final, batch_major_out):
    it = iter(refs)
    x_ref, nx_ref, wq_ref, kt_ref, v_ref, wo_ref = (next(it) for _ in range(6))
    fn_ref = next(it) if final else None
    o_ref = next(it)
    h_scr, q_scr, oh_scr, qg_scr, y_scr = (next(it) for _ in range(5))
    d_head = v_ref.shape[-1]
    chunks = d_head // LANES
    n_chunks = N_HEADS * chunks

    _norm_rows(h_scr, x_ref, nx_ref, BF16)
    _project_q(h_scr, wq_ref, q_scr, d_head)
    for b in range(nb):
        for j in range(n_chunks):
            pair, cs = _pair_slot(b, j, chunks)
            qg_scr[pair, :, cs] = q_scr[j, pl.ds(b, tt, stride=nb), :].astype(BF16)
    half = nb // 2
    for b0 in range(0, nb, half):
        pairs = slice(b0 * N_HEADS, (b0 + half) * N_HEADS)
        s = jnp.einsum("pqd,pdk->pqk", qg_scr[pairs], kt_ref[pairs], preferred_element_type=F32)
        o = jnp.einsum("pqk,pkd->pqd", _softmax(s).astype(BF16), v_ref[pairs], preferred_element_type=F32)
        for b in range(b0, b0 + half):
            for j in range(n_chunks):
                pair, cs = _pair_slot(b - b0, j, chunks)
                oh_scr[j, pl.ds(b, tt, stride=nb), :] = o[pair, :, cs]
    o_all = jnp.concatenate([oh_scr[j] for j in range(n_chunks)], axis=1)
    out = x_ref[...] + _mm(o_all.astype(BF16), wo_ref[...])

    if not (final or batch_major_out):
        o_ref[...] = out
        return
    y_scr[...] = out
    if final:
        _norm_rows(y_scr, y_scr, fn_ref, F32)
    if not batch_major_out:
        o_ref[...] = y_scr[...]
        return
    for j, cs in enumerate(_lane_chunks(y_scr.shape[-1])):
        q_scr[j] = y_scr[:, cs]
    for b in range(nb):
        for j, cs in enumerate(_lane_chunks(y_scr.shape[-1])):
            o_ref[b, :, cs] = q_scr[j, pl.ds(b, tt, stride=nb), :]


def _xattn_sample_kernel(*refs, nbs, final):
    it = iter(refs)
    x_ref, nx_ref, wq_ref, k_ref, v_ref, wo_ref = (next(it) for _ in range(6))
    fn_ref = next(it) if final else None
    o_ref = next(it)
    h_scr, q_scr, oh_scr, y_scr, qg_scr, kg_scr, vg_scr = (next(it) for _ in range(7))

    c = pl.program_id(1)
    tt = x_ref.shape[0] // nbs
    seqs = k_ref.shape[0]
    d_head = wq_ref.shape[-1] // N_HEADS
    chunks = d_head // LANES
    n_chunks = N_HEADS * chunks
    n_mem = k_ref.shape[1] // n_chunks

    @pl.when(c == 0)
    def _():
        _norm_rows(h_scr, x_ref, nx_ref, BF16)
        _project_q(h_scr, wq_ref, q_scr, d_head)

    for bl in range(seqs):
        rows = pl.ds(c * seqs + bl, tt, stride=nbs)
        for j in range(n_chunks):
            pair, cs = _pair_slot(bl, j, chunks)
            cache_rows = pl.ds((j % chunks) * N_HEADS + j // chunks, n_mem, stride=n_chunks)
            qg_scr[pair, :, cs] = q_scr[j, rows, :]
            kg_scr[pair, :, cs] = k_ref[bl, cache_rows, :].astype(BF16)
            vg_scr[pair, :, cs] = v_ref[bl, cache_rows, :].astype(BF16)
    s = jnp.einsum("pqd,pkd->pqk", qg_scr[...].astype(BF16), kg_scr[...], preferred_element_type=F32)
    o = jnp.einsum("pqk,pkd->pqd", _softmax(s).astype(BF16), vg_scr[...], preferred_element_type=F32)
    for bl in range(seqs):
        rows = pl.ds(c * seqs + bl, tt, stride=nbs)
        for j in range(n_chunks):
            pair, cs = _pair_slot(bl, j, chunks)
            oh_scr[j, rows, :] = o[pair, :, cs]

    @pl.when(c == pl.num_programs(1) - 1)
    def _():
        o_all = jnp.concatenate([oh_scr[j] for j in range(n_chunks)], axis=1)
        out = x_ref[...] + _mm(o_all.astype(BF16), wo_ref[...])
        if final:
            y_scr[...] = out
            _norm_rows(o_ref, y_scr, fn_ref, F32)
        else:
            o_ref[...] = out


def _memory_kv_kernel(*refs, n_casts):
    mem_ref, nmem_ref, wk_ref, wv_ref = refs[:4]
    cast_in = refs[4:4 + n_casts]
    kr_ref, vr_ref, kt_ref, vb_ref = refs[4 + n_casts:8 + n_casts]
    cast_out = refs[8 + n_casts:8 + 2 * n_casts]
    h_scr, wkb_scr, wvb_scr = refs[8 + 2 * n_casts:]
    for src, dst in zip(cast_in, cast_out):
        dst[...] = src[...].astype(BF16)

    @pl.when(pl.program_id(1) == 0)
    def _():
        wkb_scr[...] = wk_ref[...].astype(BF16)
        wvb_scr[...] = wv_ref[...].astype(BF16)

    n_mem = mem_ref.shape[0]
    d_head = vb_ref.shape[-1]
    chunks = d_head // LANES
    n_chunks = chunks * N_HEADS
    _norm_rows(h_scr, mem_ref, nmem_ref, BF16)
    for w_ref, r_ref in ((wkb_scr, kr_ref), (wvb_scr, vr_ref)):
        for hd in range(N_HEADS):
            kh = _mm(h_scr[...], w_ref[:, hd * d_head:(hd + 1) * d_head])
            if r_ref is kr_ref:
                kt_ref[hd] = kh.T.astype(BF16)
            else:
                vb_ref[hd] = kh.astype(BF16)
            for cc in range(chunks):
                r_ref[pl.ds(cc * N_HEADS + hd, n_mem, stride=n_chunks), :] = kh[:, cc * LANES:(cc + 1) * LANES]


def _const_spec(shape, layer=None):
    if layer is None:
        return pl.BlockSpec(shape, lambda *_: (0,) * len(shape), pipeline_mode=pl.Buffered(1))
    return pl.BlockSpec((None,) + shape, lambda *_: (layer,) + (0,) * len(shape), pipeline_mode=pl.Buffered(1))


def _row_spec(rows, cols):
    return pl.BlockSpec((rows, cols), lambda i: (i, 0))


def _tile_rows(dtype):
    return SUBLANES * WORD_BYTES // jnp.dtype(dtype).itemsize


def _buffer_bytes(shape, dtype):
    itemsize = jnp.dtype(dtype).itemsize
    dims = [s for s in shape if s is not None]
    rows = _tile_rows(dtype)
    if len(dims) >= 2:
        dims[-2] = -(-dims[-2] // rows) * rows
    dims[-1] = -(-dims[-1] // LANES) * LANES
    return math.prod(dims) * itemsize


def _pallas(body, *, grid, args, in_specs, out_specs, out_shape, scratch, value_bytes, name):
    single = not isinstance(out_specs, (list, tuple))
    o_specs, o_shapes = ([out_specs], [out_shape]) if single else (out_specs, out_shape)
    buffers = lambda spec: spec.pipeline_mode.buffer_count if spec.pipeline_mode is not None else 2
    limit = (sum(_buffer_bytes(s.block_shape, a.dtype) * buffers(s) for s, a in zip(in_specs, args))
             + sum(_buffer_bytes(s.block_shape, o.dtype) * buffers(s) for s, o in zip(o_specs, o_shapes))
             + sum(_buffer_bytes(m.shape, m.dtype) for m in scratch) + value_bytes)
    assert limit <= VMEM_BYTES, (name, limit)
    return pl.pallas_call(
        body, grid=grid, in_specs=in_specs, out_specs=out_specs, out_shape=out_shape, scratch_shapes=scratch,
        compiler_params=pltpu.CompilerParams(dimension_semantics=("arbitrary",) * len(grid), vmem_limit_bytes=limit),
        name=name)(*args)


def _chunk_major(tm, d):
    return pltpu.VMEM((d // LANES, tm, LANES), F32)


def _even_layer(x, hp, hc, p, e, l, *, nb, tt, start_pos, carry):
    batch_major_in = x.ndim == 3
    d = x.shape[-1]
    tm = nb * tt
    rows = x.shape[0] * x.shape[1] if batch_major_in else x.shape[0]
    w_a, w_b = p["pool_scale"].shape[-1], p["conv_b"].shape[-1]
    hp_rows, hc_rows = POOL_HIST * nb, CONV_HIST * nb
    assert not carry or tm >= hc_rows
    assert not batch_major_in or (carry and x.shape[0] == nb)
    n_groups = len(POOL_WINDOWS)

    args = [x, hp, hc, p["norm_mix"], p["w_in_even"], p["pool_w"], p["pool_scale"], p["conv_w"], p["conv_b"],
            p["ln_g"], p["ln_b"], p["w_out_even"]]
    hist_spec = (lambda r, c: _const_spec((r, c))) if carry else _row_spec
    x_spec = pl.BlockSpec((nb, tt, d), lambda i: (0, i, 0)) if batch_major_in else _row_spec(tm, d)
    specs = [x_spec, hist_spec(hp_rows, w_a), hist_spec(hc_rows, w_b),
             _const_spec((1, d), l), _const_spec((d, 2 * w_a + 3 * w_b), e),
             _const_spec((n_groups, w_a // n_groups, w_a // n_groups), e), _const_spec((1, w_a), e),
             _const_spec((CONV_TAPS, w_b), e), _const_spec((1, w_b), e), _const_spec((1, w_b), e),
             _const_spec((1, w_b), e), _const_spec((w_a + w_b, d), e)]

    state_spec = (lambda r, c: pl.BlockSpec((r, c), lambda i: (0, 0))) if carry else _row_spec
    n_state = 1 if carry else rows // tm
    scratch = [pltpu.VMEM((tm, d), BF16), pltpu.VMEM((hp_rows + tm, w_a), F32), pltpu.VMEM((tm, w_b), F32),
               pltpu.VMEM((tm, w_a), F32), pltpu.VMEM((tm, w_b), F32), pltpu.VMEM((tm, w_b), F32),
               pltpu.VMEM((tm, w_a + w_b), BF16), pltpu.VMEM((CONV_TAPS * SUBLANES, w_b), F32)]
    if carry:
        scratch.append(pltpu.VMEM((hc_rows, w_b), F32))
    if batch_major_in:
        scratch += [_chunk_major(tm, d), pltpu.VMEM((tm, d), F32)]
    return _pallas(
        functools.partial(_even_kernel, nb=nb, tt=tt, start_pos=start_pos, carry=carry,
                          batch_major_in=batch_major_in),
        grid=(rows // tm,), args=args, in_specs=specs,
        out_specs=[_row_spec(tm, d), state_spec(hp_rows, w_a), state_spec(hc_rows, w_b)],
        out_shape=[jax.ShapeDtypeStruct((rows, d), F32), jax.ShapeDtypeStruct((n_state * hp_rows, w_a), F32),
                   jax.ShapeDtypeStruct((n_state * hc_rows, w_b), F32)],
        scratch=scratch, value_bytes=2 * _buffer_bytes((tm, d), F32),
        name=f"even_mixer_{l}_{'carry' if carry else 'group'}")


def _odd_layer(x, hs, p, o, l, *, nb, tt, carry):
    rows, d = x.shape
    tm = nb * tt
    w_c = p["short_w"].shape[-1]
    hs_rows = SHORT_HIST * nb
    assert not carry or tm >= hs_rows

    args = [x, hs, p["norm_mix"], p["w_in_odd"], p["short_w"], p["w_out_odd"]]
    hist_spec = (lambda r, c: _const_spec((r, c))) if carry else _row_spec
    specs = [_row_spec(tm, d), hist_spec(hs_rows, w_c), _const_spec((1, d), l), _const_spec((d, 4 * w_c), o),
             _const_spec((SHORT_TAPS, w_c), o), _const_spec((w_c, d), o)]
    state_spec = (lambda r, c: pl.BlockSpec((r, c), lambda i: (0, 0))) if carry else _row_spec
    n_state = 1 if carry else rows // tm
    scratch = [pltpu.VMEM((tm, d), BF16), pltpu.VMEM((tm, w_c), F32), pltpu.VMEM((tm, w_c), BF16)]
    if carry:
        scratch.append(pltpu.VMEM((hs_rows, w_c), F32))
    return _pallas(
        functools.partial(_odd_kernel, nb=nb, tt=tt, carry=carry),
        grid=(rows // tm,), args=args, in_specs=specs,
        out_specs=[_row_spec(tm, d), state_spec(hs_rows, w_c)],
        out_shape=[jax.ShapeDtypeStruct((rows, d), F32), jax.ShapeDtypeStruct((n_state * hs_rows, w_c), F32)],
        scratch=scratch, value_bytes=2 * _buffer_bytes((tm, d), F32),
        name=f"odd_mixer_{l}_{'carry' if carry else 'group'}")


def _xattn_prompt(x, kt, vb, p, l, *, nb, tt, final_norm=None, batch_major_out=False):
    rows, d = x.shape
    tm = nb * tt
    n_pairs, d_head = nb * N_HEADS, d // N_HEADS
    assert kt.shape[1] == n_pairs and vb.shape[1] == n_pairs
    final = final_norm is not None
    args = [x, p["norm_x"], p["w_q"], kt, vb, p["w_o"]]
    specs = [_row_spec(tm, d), _const_spec((1, d), l), _const_spec((d, d), l), _const_spec(kt.shape[1:], l),
             _const_spec(vb.shape[1:], l), _const_spec((d, d), l)]
    if final:
        args, specs = args + [final_norm], specs + [_const_spec((1, d))]
    if batch_major_out:
        out_spec = pl.BlockSpec((nb, tt, d), lambda i: (0, i, 0))
        out_shape = jax.ShapeDtypeStruct((nb, rows // nb, d), F32)
    else:
        out_spec, out_shape = _row_spec(tm, d), jax.ShapeDtypeStruct((rows, d), F32)
    return _pallas(
        functools.partial(_xattn_prompt_kernel, nb=nb, tt=tt, final=final, batch_major_out=batch_major_out),
        grid=(rows // tm,), args=args, in_specs=specs, out_specs=out_spec, out_shape=out_shape,
        scratch=[pltpu.VMEM((tm, d), BF16), _chunk_major(tm, d), _chunk_major(tm, d),
                 pltpu.VMEM((n_pairs, tt, d_head), BF16), pltpu.VMEM((tm, d), F32)],
        value_bytes=4 * _buffer_bytes((tm, d), F32),
        name=f"xattn_prompt_{l}")


def _xattn_sample(x, kr, vr, p, l, *, nbs, tm, final_norm=None):
    rows, d = x.shape
    n_chunks = nbs // XATTN_SEQS
    final = final_norm is not None
    d_head = d // N_HEADS
    n_pairs = XATTN_SEQS * N_HEADS
    n_mem = kr.shape[2] * LANES // d
    cache_block = (XATTN_SEQS,) + kr.shape[2:]
    cache_spec = pl.BlockSpec((None,) + cache_block, lambda g, c: (l, g * n_chunks + c, 0, 0))
    args = [x, p["norm_x"], p["w_q"], kr, vr, p["w_o"]]
    specs = [pl.BlockSpec((tm, d), lambda g, c: (g, 0)), _const_spec((1, d), l), _const_spec((d, d), l),
             cache_spec, cache_spec, _const_spec((d, d), l)]
    if final:
        args, specs = args + [final_norm], specs + [_const_spec((1, d))]
    return _pallas(
        functools.partial(_xattn_sample_kernel, nbs=nbs, final=final),
        grid=(rows // tm, n_chunks), args=args, in_specs=specs,
        out_specs=pl.BlockSpec((tm, d), lambda g, c: (g, 0)), out_shape=jax.ShapeDtypeStruct((rows, d), F32),
        scratch=[pltpu.VMEM((tm, d), BF16), _chunk_major(tm, d), _chunk_major(tm, d), pltpu.VMEM((tm, d), F32),
                 pltpu.VMEM((n_pairs, tm // nbs, d_head), F32), pltpu.VMEM((n_pairs, n_mem, d_head), BF16),
                 pltpu.VMEM((n_pairs, n_mem, d_head), BF16)],
        value_bytes=4 * _buffer_bytes((tm, d), F32),
        name=f"xattn_sample_{l}")


def _memory_kv(mem, p, w_k, w_v, casts, n_seq, n_mem, d_head):
    d = mem.shape[-1]
    depth = w_k.shape[0]
    n_steps = depth * n_seq
    cast_specs = []
    for w in casts:
        assert w.shape[0] % (n_steps * _tile_rows(BF16)) == 0
        cast_specs.append(pl.BlockSpec((w.shape[0] // n_steps, w.shape[1]), lambda l, b: (l * n_seq + b, 0)))
    r_rows = n_mem * d // LANES
    r_spec = pl.BlockSpec((None, None, r_rows, LANES), lambda l, b: (l, b, 0, 0))
    kt_spec = pl.BlockSpec((None, None, N_HEADS, d_head, n_mem), lambda l, b: (l, b, 0, 0, 0))
    vb_spec = pl.BlockSpec((None, None, N_HEADS, n_mem, d_head), lambda l, b: (l, b, 0, 0, 0))
    w_spec = pl.BlockSpec((None, d, d), lambda l, b: (l, 0, 0))
    r_shape = jax.ShapeDtypeStruct((depth, n_seq, r_rows, LANES), F32)
    return _pallas(
        functools.partial(_memory_kv_kernel, n_casts=len(casts)),
        grid=(depth, n_seq), args=[mem, p["norm_mem"], w_k, w_v, *casts],
        in_specs=[pl.BlockSpec((n_mem, d), lambda l, b: (b, 0)), pl.BlockSpec((None, 1, d), lambda l, b: (l, 0, 0)),
                  w_spec, w_spec] + cast_specs,
        out_specs=[r_spec, r_spec, kt_spec, vb_spec] + cast_specs,
        out_shape=[r_shape, r_shape, jax.ShapeDtypeStruct((depth, n_seq, N_HEADS, d_head, n_mem), BF16),
                   jax.ShapeDtypeStruct((depth, n_seq, N_HEADS, n_mem, d_head), BF16)]
        + [jax.ShapeDtypeStruct(w.shape, BF16) for w in casts],
        scratch=[pltpu.VMEM((n_mem, d), BF16), pltpu.VMEM((d, d), BF16), pltpu.VMEM((d, d), BF16)],
        value_bytes=2 * _buffer_bytes((n_mem, d), F32),
        name="memory_kv")


def _to_time_major(x, nb):
    b, t, c = x.shape
    return x.reshape(b // nb, nb, t, c).transpose(0, 2, 1, 3).reshape(b * t, c)


def _from_time_major(y, b, t, nb):
    c = y.shape[-1]
    return y.reshape(b // nb, t, nb, c).transpose(0, 2, 1, 3).reshape(b, t, c)


def _cache_rows(c):
    *lead, n_mem, heads, d_head = c.shape
    chunks = d_head // LANES
    r = c.reshape(*lead, n_mem, heads, chunks, LANES)
    return jnp.swapaxes(r, -3, -2).reshape(*lead, n_mem * chunks * heads, LANES)


def _cache_from_rows(r, n_mem, heads, d_head):
    *lead, _, _ = r.shape
    chunks = d_head // LANES
    c = r.reshape(*lead, n_mem, chunks, heads, LANES)
    return jnp.swapaxes(c, -3, -2).reshape(*lead, n_mem, heads, d_head)


def kernel(x_prompt, x_sample, state_pool, state_conv, state_short, cache_mem_k, cache_mem_v, mem_prompt, norm_mix, w_in_even, pool_w, pool_scale, conv_w, conv_b, ln_g, ln_b, w_out_even, w_in_odd, short_w, w_out_odd, norm_x, norm_mem, w_q, w_k, w_v, w_o, final_norm):
    b_p, t_p, d = x_prompt.shape
    b_s, t_s, _ = x_sample.shape
    depth = norm_mix.shape[0]
    n_mem = mem_prompt.shape[1]
    heads, d_head = cache_mem_k.shape[-2:]
    assert heads == N_HEADS and d_head % LANES == 0 and depth >= 2
    assert d_head & (d_head - 1) == 0 and d_head.bit_length() % 2 == 1, "the score scale must be a power of two"

    nb_p, tt_p, tt_a = b_p, TILE_ROWS // b_p, ATTN_TILE_ROWS // b_p
    nb_s, tt_s = SAMPLE_TILE_ROWS // t_s, t_s
    for nb in (nb_p, nb_s):
        assert nb % SUBLANES == 0 and nb & (nb - 1) == 0
    assert t_p % tt_p == 0 and t_p % tt_a == 0 and b_s % nb_s == 0 and nb_s % XATTN_SEQS == 0

    row = lambda a: a.reshape(a.shape[0], 1, a.shape[-1])
    p = {
        "norm_mix": row(norm_mix), "norm_x": row(norm_x), "norm_mem": row(norm_mem),
        "pool_scale": row(pool_scale), "conv_w": conv_w, "conv_b": row(conv_b), "ln_g": row(ln_g), "ln_b": row(ln_b),
        "short_w": short_w,
    }
    fnorm = final_norm.reshape(1, d)
    w_a, w_b, w_c = pool_scale.shape[-1], conv_b.shape[-1], short_w.shape[-1]

    trunk = {"w_in_even": w_in_even, "pool_w": pool_w, "w_out_even": w_out_even, "w_in_odd": w_in_odd,
             "w_out_odd": w_out_odd, "w_q": w_q, "w_o": w_o}
    kr, vr, kt, vb, *cast = _memory_kv(mem_prompt.reshape(b_p * n_mem, d), p, w_k, w_v,
                                       [w.reshape(-1, w.shape[-1]) for w in trunk.values()], b_p, n_mem, d_head)
    p.update({name: c.reshape(w.shape) for (name, w), c in zip(trunk.items(), cast)})
    mem_k_p = _cache_from_rows(kr, n_mem, heads, d_head)
    mem_v_p = _cache_from_rows(vr, n_mem, heads, d_head)
    kt = kt.reshape(depth, b_p * heads, d_head, n_mem)
    vb = vb.reshape(depth, b_p * heads, n_mem, d_head)

    xp = x_prompt
    zp = jnp.zeros((POOL_HIST * nb_p, w_a), F32)
    zc = jnp.zeros((CONV_HIST * nb_p, w_b), F32)
    zs = jnp.zeros((SHORT_HIST * nb_p, w_c), F32)
    pool_p, conv_p, short_p = [], [], []
    for l in range(depth):
        last = l == depth - 1
        if l % 2 == 0:
            xp, sp, sc = _even_layer(xp, zp, zc, p, l // 2, l, nb=nb_p, tt=tt_p, start_pos=0, carry=True)
            pool_p.append(_from_time_major(sp, b_p, POOL_HIST, nb_p))
            conv_p.append(_from_time_major(sc, b_p, CONV_HIST, nb_p))
        else:
            xp, ss = _odd_layer(xp, zs, p, l // 2, l, nb=nb_p, tt=tt_p, carry=True)
            short_p.append(_from_time_major(ss, b_p, SHORT_HIST, nb_p))
        xp = _xattn_prompt(xp, kt, vb, p, l, nb=nb_p, tt=tt_a, final_norm=fnorm if last else None,
                           batch_major_out=last)
    y_prompt = xp

    krs, vrs = _cache_rows(cache_mem_k), _cache_rows(cache_mem_v)
    xs = _to_time_major(x_sample, nb_s)
    pool_s, conv_s, short_s = [], [], []
    for l in range(depth):
        if l % 2 == 0:
            e = l // 2
            xs, sp, sc = _even_layer(xs, _to_time_major(state_pool[e], nb_s), _to_time_major(state_conv[e], nb_s),
                                     p, e, l, nb=nb_s, tt=tt_s, start_pos=PAST_LEN, carry=False)
            pool_s.append(_from_time_major(sp, b_s, POOL_HIST, nb_s))
            conv_s.append(_from_time_major(sc, b_s, CONV_HIST, nb_s))
        else:
            o = l // 2
            xs, ss = _odd_layer(xs, _to_time_major(state_short[o], nb_s), p, o, l, nb=nb_s, tt=tt_s, carry=False)
            short_s.append(_from_time_major(ss, b_s, SHORT_HIST, nb_s))
        xs = _xattn_sample(xs, krs, vrs, p, l, nbs=nb_s, tm=nb_s * tt_s,
                           final_norm=fnorm if l == depth - 1 else None)
    y_sample = _from_time_major(xs, b_s, t_s, nb_s)

    return (y_prompt, y_sample, jnp.stack(pool_p), jnp.stack(conv_p), jnp.stack(short_p), mem_k_p, mem_v_p,
            jnp.stack(pool_s), jnp.stack(conv_s), jnp.stack(short_s))
```
